```python
import jax, jax.numpy as jnp
from jax import lax
import numpy as np

D_MODEL = 1024
BATCH = 8
SEQ = 8192
DEPTH = 1

HG_HEADS = 4
HG_KDIM = 128
HG_VDIM = 128
HG_WIDTH = HG_HEADS * HG_KDIM
HG_CHUNK = 64
ATT_GROUPS = ((128, 1), (512, 4), (2048, 16))
ATT_HEADS_PER_GROUP = 4
ATT_HEAD_DIM = 128
N_ATT_HEADS = ATT_HEADS_PER_GROUP * len(ATT_GROUPS)
ATT_QKV_WIDTH = N_ATT_HEADS * ATT_HEAD_DIM
ATT_WIDTH = ATT_HEADS_PER_GROUP * ATT_HEAD_DIM
ATT_BLOCK = 128
ALIBI_MAX = 8.0
COL_WIDTHS = (HG_WIDTH, HG_WIDTH, HG_HEADS * HG_VDIM, HG_HEADS * HG_VDIM,
              ATT_QKV_WIDTH, ATT_QKV_WIDTH, ATT_QKV_WIDTH, D_MODEL, D_MODEL)
IN_COLS = sum(COL_WIDTHS)
D_FF = 2816
CONV_W = 3
N_MOD = 6
EPS = 1e-6

kernel_name = "hybrid_hgrn2_dilated_attn_convffn_adaln"


def rms_norm(x, gain):
    xf = x.astype(jnp.float32)
    y = xf * lax.rsqrt(jnp.mean(xf * xf, axis=-1, keepdims=True) + EPS)
    return (y * gain.astype(jnp.float32)).astype(x.dtype)


def alibi_slopes():
    h = jnp.arange(1, N_ATT_HEADS + 1, dtype=jnp.float32)
    return jnp.exp2(-ALIBI_MAX * h / N_ATT_HEADS)


def hgrn2_mix(q_raw, f_raw, i_raw, g_raw, lb, norm_gain):
    B, S, _ = q_raw.shape
    f32 = jnp.float32
    n = S // HG_CHUNK
    q = jax.nn.silu(q_raw.astype(f32))
    fgt = lb + (1.0 - lb) * jax.nn.sigmoid(f_raw.astype(f32))
    k = 1.0 - fgt
    logf = jnp.log(fgt)
    v = i_raw.astype(f32)

    def chunks(t, dim):
        return t.reshape(B, n, HG_CHUNK, HG_HEADS, dim).transpose(1, 0, 3, 2, 4)

    qc, kc, vc = chunks(q, HG_KDIM), chunks(k, HG_KDIM), chunks(v, HG_VDIM)
    gc = jnp.cumsum(chunks(logf, HG_KDIM), axis=3)
    causal = jnp.tril(jnp.ones((HG_CHUNK, HG_CHUNK), dtype=bool))

    def step(state, xs):
        q_, k_, v_, g_ = xs
        diff = g_[:, :, :, None, :] - g_[:, :, None, :, :]
        decay = jnp.exp(jnp.where(causal[:, :, None], diff, -jnp.inf))
        scores = jnp.einsum('bhtk,bhsk,bhtsk->bhts', q_, k_, decay)
        o = (jnp.einsum('bhts,bhsv->bhtv', scores, v_)
             + jnp.einsum('bhtk,bhkv->bhtv', q_ * jnp.exp(g_), state))
        g_last = g_[:, :, -1, :]
        k_dec = k_ * jnp.exp(g_last[:, :, None, :] - g_)
        state = jnp.exp(g_last)[..., None] * state + jnp.einsum('bhsk,bhsv->bhkv', k_dec, v_)
        return state, o

    state0 = jnp.zeros((B, HG_HEADS, HG_KDIM, HG_VDIM), f32)
    _, o = lax.scan(step, state0, (qc, kc, vc, gc))
    o = o.transpose(1, 0, 3, 2, 4).reshape(B, S, HG_HEADS, HG_VDIM)
    o = rms_norm(o, norm_gain).reshape(B, S, HG_HEADS * HG_VDIM)
    return (o * jax.nn.silu(g_raw.astype(f32))).astype(q_raw.dtype)


def dilated_group(q, k, v, slopes, window, dilation):
    B, S, H, E = q.shape
    band = window // dilation
    L = S // dilation
    nb = -(-L // ATT_BLOCK)
    Lp = nb * ATT_BLOCK

    def to_sub(t):
        t = t.reshape(B, L, dilation, H, E).transpose(0, 2, 1, 3, 4)
        t = jnp.pad(t, ((0, 0), (0, 0), (0, Lp - L), (0, 0), (0, 0)))
        return t.reshape(B, dilation, nb, ATT_BLOCK, H, E)

    def with_prev(t):
        prev = jnp.pad(t, ((0, 0), (0, 0), (1, 0), (0, 0), (0, 0), (0, 0)))[:, :, :-1]
        return jnp.concatenate([prev, t], axis=3)

    qb = to_sub(q)
    kw = with_prev(to_sub(k))
    vw = with_prev(to_sub(v))

    qi = jnp.arange(ATT_BLOCK)[:, None] + ATT_BLOCK
    ki = jnp.arange(2 * ATT_BLOCK)[None, :]
    dist = qi - ki
    blk = jnp.arange(nb)[:, None, None]
    valid = (dist >= 0)[None] & (dist <= band)[None] & (blk * ATT_BLOCK + ki[None] - ATT_BLOCK >= 0)
    bias = -slopes[:, None, None] * (dist.astype(jnp.float32) * dilation)[None]

    scores = jnp.einsum('bdnqhe,bdnkhe->bdnhqk', qb, kw).astype(jnp.float32) * (E ** -0.5)
    scores = jnp.where(valid[None, None, :, None], scores + bias[None, None, None], -jnp.inf)
    m = jnp.max(scores, axis=-1, keepdims=True)
    p = jnp.exp(scores - m)
    s = jnp.sum(p, axis=-1, keepdims=True)
    o = jnp.einsum('bdnhqk,bdnkhe->bdnqhe', p / s, vw.astype(jnp.float32))
    lse = (m + jnp.log(s))[..., 0]

    o = o.reshape(B, dilation, Lp, H, E)[:, :, :L].transpose(0, 2, 1, 3, 4).reshape(B, S, H, E)
    lse = lse.transpose(0, 1, 2, 4, 3).reshape(B, dilation, Lp, H)[:, :, :L]
    lse = lse.transpose(0, 2, 1, 3).reshape(B, S, H)
    return o, lse


def dilated_mixture(q_flat, k_flat, v_flat):
    B, S, _ = q_flat.shape
    shp = (B, S, N_ATT_HEADS, ATT_HEAD_DIM)
    q, k, v = q_flat.reshape(shp), k_flat.reshape(shp), v_flat.reshape(shp)
    slopes = alibi_slopes()
    outs, lses = [], []
    for g, (window, dilation) in enumerate(ATT_GROUPS):
        sl = slice(g * ATT_HEADS_PER_GROUP, (g + 1) * ATT_HEADS_PER_GROUP)
        o, lse = dilated_group(q[:, :, sl], k[:, :, sl], v[:, :, sl], slopes[sl], window, dilation)
        outs.append(o)
        lses.append(lse)
    w = jax.nn.softmax(jnp.stack(lses, axis=0), axis=0)
    o = jnp.sum(w[..., None] * jnp.stack(outs, axis=0), axis=0)
    return o.reshape(B, S, ATT_WIDTH).astype(q_flat.dtype)


def causal_dwconv(u, w, b):
    S = u.shape[1]
    up = jnp.pad(u, ((0, 0), (CONV_W - 1, 0), (0, 0)))
    y = b
    for j in range(CONV_W):
        y = y + up[:, j:j + S] * w[j]
    return y


def setup_inputs(seed: int = 0) -> dict:
    key = jax.random.key(seed)
    ks = jax.random.split(key, 18)
    nrm = jax.random.normal
    f32 = jnp.float32
    D = D_MODEL
    return {
        "x": nrm(ks[0], (BATCH, SEQ, D), f32),
        "c": nrm(ks[1], (BATCH, D), f32),
        "w_ada": nrm(ks[2], (DEPTH, D, N_MOD * D), f32) * (0.5 * D ** -0.5),
        "b_ada": nrm(ks[3], (DEPTH, N_MOD * D), f32) * 0.01,
        "g_norm_mix": 1.0 + 0.02 * nrm(ks[4], (DEPTH, D), f32),
        "w_in": nrm(ks[5], (DEPTH, D, IN_COLS), f32) * D ** -0.5,
        "lb_logits": 0.1 * nrm(ks[6], (DEPTH + 1, HG_WIDTH), f32),
        "g_hg_norm": 1.0 + 0.02 * nrm(ks[7], (DEPTH, HG_HEADS, HG_VDIM), f32),
        "w_branch_a": nrm(ks[8], (DEPTH, HG_HEADS * HG_VDIM, D), f32) * (HG_HEADS * HG_VDIM) ** -0.5,
        "w_branch_b": nrm(ks[9], (DEPTH, ATT_WIDTH, D), f32) * ATT_WIDTH ** -0.5,
        "w_out": nrm(ks[10], (DEPTH, D, D), f32) * D ** -0.5,
        "g_norm_ffn": 1.0 + 0.02 * nrm(ks[11], (DEPTH, D), f32),
        "w_up": nrm(ks[12], (DEPTH, D, 2 * D_FF), f32) * D ** -0.5,
        "conv_w": nrm(ks[13], (DEPTH, CONV_W, D_FF), f32) * CONV_W ** -0.5,
        "conv_b": 0.01 * nrm(ks[14], (DEPTH, D_FF), f32),
        "w_down": nrm(ks[15], (DEPTH, D_FF, D), f32) * D_FF ** -0.5,
        "g_final": 1.0 + 0.02 * nrm(ks[16], (D,), f32),
    }


def reference(x, c, w_ada, b_ada, g_norm_mix, w_in, lb_logits, g_hg_norm, w_branch_a, w_branch_b,
              w_out, g_norm_ffn, w_up, conv_w, conv_b, w_down, g_final):
    split_idx = [int(v) for v in np.cumsum(COL_WIDTHS)[:-1]]
    lower_bounds = jnp.cumsum(jax.nn.softmax(lb_logits.astype(jnp.float32), axis=0), axis=0)
    for l in range(DEPTH):
        mod = jax.nn.silu(c) @ w_ada[l] + b_ada[l]
        sh1, sc1, gt1, sh2, sc2, gt2 = [m[:, None, :] for m in jnp.split(mod, N_MOD, axis=-1)]

        h = rms_norm(x, g_norm_mix[l]) * (1.0 + sc1) + sh1
        proj = h @ w_in[l]
        hq, hf, hi, hg, aq, ak, av, ga, gb = jnp.split(proj, split_idx, axis=-1)
        y_a = hgrn2_mix(hq, hf, hi, hg, lower_bounds[l], g_hg_norm[l])
        y_b = dilated_mixture(aq, ak, av)
        merged = jax.nn.sigmoid(ga) * (y_a @ w_branch_a[l]) + jax.nn.sigmoid(gb) * (y_b @ w_branch_b[l])
        x = x + (1.0 + gt1) * (merged @ w_out[l])

        h2 = rms_norm(x, g_norm_ffn[l]) * (1.0 + sc2) + sh2
        u_gate, u_val = jnp.split(h2 @ w_up[l], 2, axis=-1)
        u_gate = causal_dwconv(u_gate, conv_w[l], conv_b[l])
        x = x + (1.0 + gt2) * ((jax.nn.gelu(u_gate) * u_val) @ w_down[l])
    return rms_norm(x, g_final)
```

```python
import functools

import numpy as np
import jax
import jax.numpy as jnp
from jax import lax
from jax.experimental import pallas as pl
from jax.experimental.pallas import tpu as pltpu

F32 = jnp.float32
BF16 = jnp.bfloat16

D_MODEL = 1024
HG_HEADS = 4
HG_DIM = 128
HG_WIDTH = HG_HEADS * HG_DIM
ATT_GROUPS = ((128, 1), (512, 4), (2048, 16))
ATT_HPG = 4
ATT_E = 128
N_ATT_HEADS = ATT_HPG * len(ATT_GROUPS)
ATT_GW = ATT_HPG * ATT_E
ATT_BLOCK = 128
ALIBI_MAX = 8.0
D_FF = 2816
N_MOD = 6
EPS = 1e-6

COLB_GA = 0
COLB_GB = 2
COLB_HG = 4
COLB_AQ = 8
COLB_AK = 11
COLB_AV = 14
IN_COLS = 17 * 512

VMEM_LIMIT_BYTES = 56 * 1024 * 1024

HG_CHUNK = 128
FFN_CHUNK = 256


def _sigmoid(x):
    return 1.0 / (1.0 + jnp.exp(-x))


def _silu(x):
    return x * _sigmoid(x)


def _dot(a, b):
    return jnp.dot(a, b, preferred_element_type=F32)


def _dot_nt(a, b):
    return lax.dot_general(a, b, (((1,), (1,)), ((), ())), preferred_element_type=F32)


def _dot_tn(a, b):
    return lax.dot_general(a, b, (((0,), (0,)), ((), ())), preferred_element_type=F32)


def _params(*sem):
    return pltpu.CompilerParams(dimension_semantics=sem, vmem_limit_bytes=VMEM_LIMIT_BYTES)


def _mod_kernel(c_ref, w_ref, b_ref, o_ref):
    s = _silu(c_ref[...]).astype(BF16)
    o_ref[...] = _dot(s, w_ref[...].astype(BF16)) + b_ref[...]


def _modulation(c, w, b):
    bsz, d = c.shape
    n = w.shape[1]
    tn = 1024
    return pl.pallas_call(
        _mod_kernel,
        grid=(n // tn,),
        in_specs=[pl.BlockSpec((bsz, d), lambda j: (0, 0)),
                  pl.BlockSpec((d, tn), lambda j: (0, j)),
                  pl.BlockSpec((1, tn), lambda j: (0, j))],
        out_specs=pl.BlockSpec((bsz, tn), lambda j: (0, j)),
        out_shape=jax.ShapeDtypeStruct((bsz, n), F32),
        compiler_params=_params("arbitrary"),
        name="modulation",
    )(c, w, b.reshape(1, n))


def _inproj_kernel(x_ref, g_ref, sc_ref, sh_ref, w_ref, o_ref, h_scr):
    @pl.when(pl.program_id(2) == 0)
    def _():
        x = x_ref[0]
        ms = jnp.mean(x * x, axis=-1, keepdims=True)
        y = x * lax.rsqrt(ms + EPS) * g_ref[...]
        h_scr[...] = (y * (1.0 + sc_ref[0]) + sh_ref[0]).astype(BF16)

    o_ref[0] = _dot(h_scr[...], w_ref[...]).astype(o_ref.dtype)


def _in_projection(x, gain, sc, sh, w):
    bsz, seq, d = x.shape
    n = w.shape[1]
    tm, tn = 1024, n // 4
    return pl.pallas_call(
        _inproj_kernel,
        grid=(bsz, seq // tm, n // tn),
        in_specs=[pl.BlockSpec((1, tm, d), lambda b, i, j: (b, i, 0)),
                  pl.BlockSpec((1, d), lambda b, i, j: (0, 0)),
                  pl.BlockSpec((1, 1, d), lambda b, i, j: (b, 0, 0)),
                  pl.BlockSpec((1, 1, d), lambda b, i, j: (b, 0, 0)),
                  pl.BlockSpec((d, tn), lambda b, i, j: (0, j))],
        out_specs=pl.BlockSpec((1, tm, tn), lambda b, i, j: (b, i, j)),
        out_shape=jax.ShapeDtypeStruct((bsz, seq, n), BF16),
        scratch_shapes=[pltpu.VMEM((tm, d), BF16)],
        compiler_params=_params("arbitrary", "arbitrary", "arbitrary"),
        name="in_projection",
    )(x, gain.reshape(1, d), sc, sh, w)


def _hgrn_level_table(c):
    t = np.arange(c)[:, None]
    s = np.arange(c)[None, :]
    x = t ^ s
    lvl = np.where(x > 0, np.floor(np.log2(np.maximum(x, 1))).astype(np.int32), 0)
    n_levels = int(np.log2(c))
    out = np.where(t > s, lvl, np.where(t == s, n_levels, -1)).astype(np.int32)
    return out


def _hgrn_kernel(q_ref, f_ref, i_ref, gate_ref, lbl_ref, gain_ref, lvl_ref, o_ref, st_scr, g_scr, *, layer):
    c = q_ref.shape[1]
    n_levels = c.bit_length() - 1

    @pl.when(pl.program_id(1) == 0)
    def _():
        st_scr[...] = jnp.zeros_like(st_scr)

    a = lbl_ref[...]
    e = jnp.exp(a - jnp.max(a, axis=0, keepdims=True))
    lb = jnp.sum(e[:layer + 1], axis=0, keepdims=True) / jnp.sum(e, axis=0, keepdims=True)

    q = _silu(q_ref[0].astype(F32))
    fg = lb + (1.0 - lb) * _sigmoid(f_ref[0].astype(F32))
    kk = 1.0 - fg
    logf = jnp.log(fg)

    r2 = lax.broadcasted_iota(jnp.int32, (c, c), 0)
    c2 = lax.broadcasted_iota(jnp.int32, (c, c), 1)
    tril = jnp.where(r2 >= c2, 1.0, 0.0).astype(BF16)
    hi = logf.astype(BF16)
    lo = (logf - hi.astype(F32)).astype(BF16)
    g = _dot(tril, hi) + _dot(tril, lo)
    g_scr[...] = g

    row = lax.broadcasted_iota(jnp.int32, (c, HG_DIM), 0)
    exps, rowbs = [], []
    for li in range(n_levels):
        h = 1 << li
        if h >= 4:
            parts = [jnp.broadcast_to(g_scr[pl.ds(p * 2 * h + h - 1, 1), :], (2 * h, HG_WIDTH))
                     for p in range(c // (2 * h))]
            e_l = -jnp.abs(g - jnp.concatenate(parts, axis=0))
            rowb = (row & (2 * h - 1)) >= h
        elif h == 2:
            up = pltpu.roll(logf, 1, 0)
            dn = pltpu.roll(logf, c - 1, 0)
            r4 = lax.broadcasted_iota(jnp.int32, (c, HG_WIDTH), 0) & 3
            e_l = jnp.where(r4 == 0, dn, jnp.where(r4 == 1, 0.0, jnp.where(r4 == 2, logf, up + logf)))
            rowb = (row & 3) >= 2
        else:
            r2w = lax.broadcasted_iota(jnp.int32, (c, HG_WIDTH), 0) & 1
            e_l = jnp.where(r2w == 1, logf, 0.0)
            rowb = (row & 1) == 1
        exps.append(jnp.exp(e_l))
        rowbs.append(rowb)

    eg = jnp.exp(g)
    g_last = g_scr[pl.ds(c - 1, 1), :]
    e_last = jnp.exp(g_last)
    kdec = kk * jnp.exp(g_last - g)
    lvl = lvl_ref[...]
    gate = _silu(gate_ref[0].astype(F32))
    gain = gain_ref[...]

    for hd in range(HG_HEADS):
        sl = slice(hd * HG_DIM, (hd + 1) * HG_DIM)
        qh, kh = q[:, sl], kk[:, sl]
        scores = jnp.where(lvl == n_levels, _dot_nt(qh.astype(BF16), kh.astype(BF16)), 0.0)
        for li in range(n_levels):
            xl = (jnp.where(rowbs[li], qh, kh) * exps[li][:, sl]).astype(BF16)
            scores = jnp.where(lvl == li, _dot_nt(xl, xl), scores)
        v = i_ref[0, :, sl]
        st = st_scr[hd]
        o = _dot(scores.astype(BF16), v)
        o = o + _dot_nt((qh * eg[:, sl]).astype(BF16), st.astype(BF16))
        st_scr[hd] = st * e_last[:, sl] + _dot_tn(v, kdec[:, sl].astype(BF16))
        ms = jnp.mean(o * o, axis=-1, keepdims=True)
        y = o * lax.rsqrt(ms + EPS) * gain[:, sl]
        o_ref[0, :, sl] = (y * gate[:, sl]).astype(o_ref.dtype)


def _hgrn2(proj, lb_logits, gain, layer):
    bsz, seq, _ = proj.shape
    c = HG_CHUNK
    lvl = jnp.asarray(_hgrn_level_table(c))

    def col(k):
        return pl.BlockSpec((1, c, HG_WIDTH), lambda b, n, k=k: (b, n, COLB_HG + k))

    return pl.pallas_call(
        functools.partial(_hgrn_kernel, layer=layer),
        grid=(bsz, seq // c),
        in_specs=[col(0), col(1), col(2), col(3),
                  pl.BlockSpec(lb_logits.shape, lambda b, n: (0, 0)),
                  pl.BlockSpec((1, HG_WIDTH), lambda b, n: (0, 0)),
                  pl.BlockSpec((c, c), lambda b, n: (0, 0))],
        out_specs=pl.BlockSpec((1, c, HG_WIDTH), lambda b, n: (b, n, 0)),
        out_shape=jax.ShapeDtypeStruct((bsz, seq, HG_WIDTH), BF16),
        scratch_shapes=[pltpu.VMEM((HG_HEADS, HG_DIM, HG_DIM), F32),
                        pltpu.VMEM((c, HG_WIDTH), F32)],
        compiler_params=_params("arbitrary", "arbitrary"),
        name="hgrn2",
    )(proj, proj, proj, proj, lb_logits, gain.reshape(1, HG_WIDTH), lvl)


def _attn_kernel(q_ref, kp_ref, kc_ref, vp_ref, vc_ref, o_ref, lse_ref, *, slopes, dilation, band):
    blk = ATT_BLOCK
    n = pl.program_id(2)
    qi = lax.broadcasted_iota(jnp.int32, (blk, 2 * blk), 0) + blk
    ki = lax.broadcasted_iota(jnp.int32, (blk, 2 * blk), 1)
    dist = qi - ki
    first_key = jnp.where(n > 0, 0, blk)
    valid = (dist >= 0) & (dist <= band) & (ki >= first_key)
    distf = dist.astype(F32) * float(dilation)
    for j in range(ATT_HPG):
        sl = slice(j * ATT_E, (j + 1) * ATT_E)
        kw = jnp.concatenate([kp_ref[0, :, sl], kc_ref[0, :, sl]], axis=0)
        vw = jnp.concatenate([vp_ref[0, :, sl], vc_ref[0, :, sl]], axis=0)
        s = _dot_nt(q_ref[0, :, sl], kw) * (ATT_E ** -0.5)
        s = jnp.where(valid, s - slopes[j] * distf, -jnp.inf)
        m = jnp.max(s, axis=-1, keepdims=True)
        p = jnp.exp(s - m)
        l = jnp.sum(p, axis=-1, keepdims=True)
        o = _dot(p.astype(BF16), vw) / l
        o_ref[0, :, sl] = o.astype(o_ref.dtype)
        lse_ref[0, :, sl] = jnp.broadcast_to(m + jnp.log(l), (blk, ATT_E))


def _dilated_group(proj, group):
    bsz, seq, ncols = proj.shape
    window, d = ATT_GROUPS[group]
    band = window // d
    assert band <= ATT_BLOCK and seq % (d * ATT_BLOCK) == 0
    lsub = seq // d
    nb = lsub // ATT_BLOCK
    cb = ncols // ATT_GW
    view = proj.reshape(bsz, lsub, d * ncols)
    heads = np.arange(group * ATT_HPG, (group + 1) * ATT_HPG) + 1
    slopes = tuple(float(s) for s in np.exp2(-ALIBI_MAX * heads / N_ATT_HEADS))

    def cur(base):
        return pl.BlockSpec((1, ATT_BLOCK, ATT_GW), lambda b, r, n: (b, n, r * cb + base + group))

    def prev(base):
        return pl.BlockSpec((1, ATT_BLOCK, ATT_GW),
                            lambda b, r, n: (b, jnp.maximum(n - 1, 0), r * cb + base + group))

    o, lse = pl.pallas_call(
        functools.partial(_attn_kernel, slopes=slopes, dilation=d, band=band),
        grid=(bsz, d, nb),
        in_specs=[cur(COLB_AQ), prev(COLB_AK), cur(COLB_AK), prev(COLB_AV), cur(COLB_AV)],
        out_specs=[pl.BlockSpec((1, ATT_BLOCK, ATT_GW), lambda b, r, n: (b, n, r)),
                   pl.BlockSpec((1, ATT_BLOCK, ATT_GW), lambda b, r, n: (b, n, r))],
        out_shape=[jax.ShapeDtypeStruct((bsz, lsub, d * ATT_GW), BF16),
                   jax.ShapeDtypeStruct((bsz, lsub, d * ATT_GW), F32)],
        compiler_params=_params("arbitrary", "arbitrary", "arbitrary"),
        name=f"dilated_attn_g{group}",
    )(view, view, view, view, view)
    return o.reshape(bsz, seq, ATT_GW), lse.reshape(bsz, seq, ATT_GW)


def _mix_kernel(ya_ref, o1_ref, o2_ref, o3_ref, l1_ref, l2_ref, l3_ref, ga_ref, gb_ref, x_ref,
                wa_ref, wb_ref, wo_ref, gt_ref, sc_ref, sh_ref, gn_ref, x1_ref, h2_ref):
    l1, l2, l3 = l1_ref[0], l2_ref[0], l3_ref[0]
    m = jnp.maximum(jnp.maximum(l1, l2), l3)
    e1, e2, e3 = jnp.exp(l1 - m), jnp.exp(l2 - m), jnp.exp(l3 - m)
    yb = (e1 * o1_ref[0].astype(F32) + e2 * o2_ref[0].astype(F32) + e3 * o3_ref[0].astype(F32)) / (e1 + e2 + e3)
    za = _dot(ya_ref[0], wa_ref[...])
    zb = _dot(yb.astype(BF16), wb_ref[...])
    merged = _sigmoid(ga_ref[0].astype(F32)) * za + _sigmoid(gb_ref[0].astype(F32)) * zb
    x1 = x_ref[0] + (1.0 + gt_ref[0]) * _dot(merged.astype(BF16), wo_ref[...])
    x1_ref[0] = x1
    ms = jnp.mean(x1 * x1, axis=-1, keepdims=True)
    y = x1 * lax.rsqrt(ms + EPS) * gn_ref[...]
    h2_ref[0] = (y * (1.0 + sc_ref[0]) + sh_ref[0]).astype(h2_ref.dtype)


def _mix(ya, outs, lses, proj, x, wa, wb, wo, gt, sc, sh, gn):
    bsz, seq, d = x.shape
    tm = 512

    def tok(width, colb=0):
        return pl.BlockSpec((1, tm, width), lambda b, i, colb=colb: (b, i, colb))

    def full(shape):
        return pl.BlockSpec(shape, lambda b, i: (0,) * len(shape), pipeline_mode=pl.Buffered(1))

    def per_batch():
        return pl.BlockSpec((1, 1, d), lambda b, i: (b, 0, 0))

    return pl.pallas_call(
        _mix_kernel,
        grid=(bsz, seq // tm),
        in_specs=[tok(HG_WIDTH)] + [tok(ATT_GW)] * 6 + [tok(d, COLB_GA // 2), tok(d, COLB_GB // 2), tok(d),
                  full(wa.shape), full(wb.shape), full(wo.shape),
                  per_batch(), per_batch(), per_batch(), full((1, d))],
        out_specs=[tok(d), tok(d)],
        out_shape=[jax.ShapeDtypeStruct((bsz, seq, d), F32), jax.ShapeDtypeStruct((bsz, seq, d), BF16)],
        compiler_params=_params("arbitrary", "arbitrary"),
        name="mix",
    )(ya, *outs, *lses, proj, proj, x, wa, wb, wo, gt, sc, sh, gn.reshape(1, d))


def _ffn_kernel(h_ref, x_ref, wu_ref, wd_ref, cw_ref, cb_ref, gt_ref, gf_ref, o_ref, carry_scr, acc_scr, *,
                final_norm):
    tm = h_ref.shape[1]
    fc = FFN_CHUNK

    @pl.when(pl.program_id(1) == 0)
    def _():
        carry_scr[...] = jnp.zeros_like(carry_scr)

    h = h_ref[0]
    row = lax.broadcasted_iota(jnp.int32, (tm, fc), 0)
    for ci in range(D_FF // fc):
        cs = slice(ci * fc, (ci + 1) * fc)
        ug = _dot(h, wu_ref[:, cs])
        uv = _dot(h, wu_ref[:, D_FF + ci * fc:D_FF + (ci + 1) * fc])
        prev2 = jnp.broadcast_to(carry_scr[6:7, cs], (tm, fc))
        prev1 = jnp.broadcast_to(carry_scr[7:8, cs], (tm, fc))
        r1 = pltpu.roll(ug, 1, 0)
        r2 = pltpu.roll(ug, 2, 0)
        u1 = jnp.where(row == 0, prev1, r1)
        u2 = jnp.where(row == 0, prev2, jnp.where(row == 1, prev1, r2))
        carry_scr[:, cs] = ug[tm - 8:tm, :]
        y = cb_ref[:, cs] + u2 * cw_ref[0:1, cs] + u1 * cw_ref[1:2, cs] + ug * cw_ref[2:3, cs]
        gel = 0.5 * y * (1.0 + jnp.tanh(0.7978845608028654 * (y + 0.044715 * (y * y * y))))
        part = _dot((gel * uv).astype(BF16), wd_ref[cs, :])
        if ci == 0:
            acc_scr[...] = part
        else:
            acc_scr[...] += part
    x2 = x_ref[0] + (1.0 + gt_ref[0]) * acc_scr[...]
    if final_norm:
        ms = jnp.mean(x2 * x2, axis=-1, keepdims=True)
        x2 = x2 * lax.rsqrt(ms + EPS) * gf_ref[...]
    o_ref[0] = x2


def _ffn(h2, x1, wu, wd, cw, cb, gt, gf, final_norm):
    bsz, seq, d = x1.shape
    tm = 512

    def tok():
        return pl.BlockSpec((1, tm, d), lambda b, i: (b, i, 0))

    def full(shape):
        return pl.BlockSpec(shape, lambda b, i: (0,) * len(shape), pipeline_mode=pl.Buffered(1))

    return pl.pallas_call(
        functools.partial(_ffn_kernel, final_norm=final_norm),
        grid=(bsz, seq // tm),
        in_specs=[tok(), tok(), full(wu.shape), full(wd.shape), full(cw.shape), full((1, D_FF)),
                  pl.BlockSpec((1, 1, d), lambda b, i: (b, 0, 0)), full((1, d))],
        out_specs=tok(),
        out_shape=jax.ShapeDtypeStruct((bsz, seq, d), F32),
        scratch_shapes=[pltpu.VMEM((8, D_FF), F32), pltpu.VMEM((tm, d), F32)],
        compiler_params=_params("arbitrary", "arbitrary"),
        name="ffn",
    )(h2, x1, wu, wd, cw, cb.reshape(1, D_FF), gt, gf.reshape(1, d))


def kernel(x, c, w_ada, b_ada, g_norm_mix, w_in, lb_logits, g_hg_norm, w_branch_a, w_branch_b, w_out, g_norm_ffn,
           w_up, conv_w, conv_b, w_down, g_final):
    depth = w_ada.shape[0]
    bsz = x.shape[0]
    n_hg = 4 * HG_WIDTH
    for l in range(depth):
        mod = _modulation(c, w_ada[l], b_ada[l]).reshape(bsz, N_MOD, 1, D_MODEL)
        sh1, sc1, gt1, sh2, sc2, gt2 = [mod[:, k] for k in range(N_MOD)]
        w = w_in[l]
        n_att = 3 * ATT_GW * len(ATT_GROUPS)
        w_perm = jnp.concatenate([w[:, n_hg + n_att:], w[:, :n_hg + n_att]], axis=1).astype(BF16)

        proj = _in_projection(x, g_norm_mix[l], sc1, sh1, w_perm)
        ya = _hgrn2(proj, lb_logits, g_hg_norm[l].reshape(HG_WIDTH), l)
        groups = [_dilated_group(proj, g) for g in range(len(ATT_GROUPS))]
        x1, h2 = _mix(ya, [o for o, _ in groups], [s for _, s in groups], proj, x,
                      w_branch_a[l].astype(BF16), w_branch_b[l].astype(BF16), w_out[l].astype(BF16),
                      gt1, sc2, sh2, g_norm_ffn[l])
        x = _ffn(h2, x1, w_up[l].astype(BF16), w_down[l].astype(BF16), conv_w[l], conv_b[l], gt2, g_final,
                 final_norm=(l == depth - 1))
    return x
```

```python
import functools

import numpy as np
import jax
import jax.numpy as jnp
from jax import lax
from jax.experimental import pallas as pl
from jax.experimental.pallas import tpu as pltpu

F32 = jnp.float32
BF16 = jnp.bfloat16

D_MODEL = 1024
HG_HEADS = 4
HG_DIM = 128
HG_WIDTH = HG_HEADS * HG_DIM
ATT_GROUPS = ((128, 1), (512, 4), (2048, 16))
ATT_HPG = 4
ATT_E = 128
N_ATT_HEADS = ATT_HPG * len(ATT_GROUPS)
ATT_GW = ATT_HPG * ATT_E
ATT_BLOCK = 128
ALIBI_MAX = 8.0
D_FF = 2816
N_MOD = 6
EPS = 1e-6

COLB_GA = 0
COLB_GB = 2
COLB_HG = 4
N_MAIN = 8 * 512
N_QKV = 3 * ATT_GW
ATT_SUPER = 2048
LANES = 128

VMEM_LIMIT_BYTES = 56 * 1024 * 1024

HG_CHUNK = 128
FFN_CHUNK = 256


def _sigmoid(x):
    return 1.0 / (1.0 + jnp.exp(-x))


def _silu(x):
    return x * _sigmoid(x)


def _dot(a, b):
    return jnp.dot(a, b, preferred_element_type=F32)


def _dot_nt(a, b):
    return lax.dot_general(a, b, (((1,), (1,)), ((), ())), preferred_element_type=F32)


def _dot_tn(a, b):
    return lax.dot_general(a, b, (((0,), (0,)), ((), ())), preferred_element_type=F32)


def _params(*sem):
    return pltpu.CompilerParams(dimension_semantics=sem, vmem_limit_bytes=VMEM_LIMIT_BYTES)


def _mod_kernel(c_ref, w_ref, b_ref, o_ref):
    s = _silu(c_ref[...]).astype(BF16)
    o_ref[...] = _dot(s, w_ref[...].astype(BF16)) + b_ref[...]


def _modulation(c, w, b):
    bsz, d = c.shape
    n = w.shape[1]
    tn = 1024
    return pl.pallas_call(
        _mod_kernel,
        grid=(n // tn,),
        in_specs=[pl.BlockSpec((bsz, d), lambda j: (0, 0)),
                  pl.BlockSpec((d, tn), lambda j: (0, j)),
                  pl.BlockSpec((1, tn), lambda j: (0, j))],
        out_specs=pl.BlockSpec((bsz, tn), lambda j: (0, j)),
        out_shape=jax.ShapeDtypeStruct((bsz, n), F32),
        compiler_params=_params("arbitrary"),
        name="modulation",
    )(c, w, b.reshape(1, n))


def _inproj_kernel(x_ref, g_ref, sc_ref, sh_ref, w_ref, main_ref, a0_ref, a1_ref, a2_ref, slab_scr, hp1_scr, hp2_scr):
    tm, d_model = x_ref.shape[1], x_ref.shape[2]
    x = x_ref[0]
    ms = jnp.mean(x * x, axis=-1, keepdims=True)
    y = x * lax.rsqrt(ms + EPS) * g_ref[...]
    h = y * (1.0 + sc_ref[0]) + sh_ref[0]
    hb = h.astype(BF16)
    for c0 in range(0, N_MAIN, 1024):
        main_ref[0, :, c0:c0 + 1024] = _dot(hb, w_ref[:, c0:c0 + 1024]).astype(main_ref.dtype)
    a0_ref[0, 0] = _dot(hb, w_ref[:, N_MAIN:N_MAIN + N_QKV]).astype(a0_ref.dtype)

    n_slabs = d_model // LANES
    for s in range(n_slabs):
        slab_scr[s] = h[:, s * LANES:(s + 1) * LANES]
    for gi, (out_ref, hp_scr) in enumerate(((a1_ref, hp1_scr), (a2_ref, hp2_scr)), start=1):
        dil = ATT_GROUPS[gi][1]
        rows = tm // dil
        for r in range(dil):
            piece = jnp.concatenate([slab_scr[s, pl.ds(r, rows, stride=dil), :] for s in range(n_slabs)], axis=1)
            hp_scr[r * rows:(r + 1) * rows, :] = piece.astype(BF16)
        c0 = N_MAIN + gi * N_QKV
        res = _dot(hp_scr[...], w_ref[:, c0:c0 + N_QKV]).astype(out_ref.dtype)
        for r in range(dil):
            out_ref[0, r] = res[r * rows:(r + 1) * rows, :]


def _in_projection(x, gain, sc, sh, w):
    bsz, seq, d = x.shape
    tm = 512
    dils = [dil for _, dil in ATT_GROUPS]
    assert dils[0] == 1 and all(tm % (16 * dil) == 0 for dil in dils)

    def full(shape):
        return pl.BlockSpec(shape, lambda b, i: (0,) * len(shape), pipeline_mode=pl.Buffered(1))

    out_shape = [jax.ShapeDtypeStruct((bsz, seq, N_MAIN), BF16)]
    out_specs = [pl.BlockSpec((1, tm, N_MAIN), lambda b, i: (b, i, 0))]
    for dil in dils:
        out_shape.append(jax.ShapeDtypeStruct((bsz, dil, seq // dil, N_QKV), BF16))
        out_specs.append(pl.BlockSpec((1, dil, tm // dil, N_QKV), lambda b, i: (b, 0, i, 0)))
    return pl.pallas_call(
        _inproj_kernel,
        grid=(bsz, seq // tm),
        in_specs=[pl.BlockSpec((1, tm, d), lambda b, i: (b, i, 0)),
                  full((1, d)),
                  pl.BlockSpec((1, 1, d), lambda b, i: (b, 0, 0)),
                  pl.BlockSpec((1, 1, d), lambda b, i: (b, 0, 0)),
                  full(w.shape)],
        out_specs=out_specs,
        out_shape=out_shape,
        scratch_shapes=[pltpu.VMEM((d // LANES, tm, LANES), F32),
                        pltpu.VMEM((tm, d), BF16), pltpu.VMEM((tm, d), BF16)],
        compiler_params=_params("arbitrary", "arbitrary"),
        name="in_projection",
    )(x, gain.reshape(1, d), sc, sh, w)


def _hgrn_level_table(c):
    t = np.arange(c)[:, None]
    s = np.arange(c)[None, :]
    x = t ^ s
    lvl = np.where(x > 0, np.floor(np.log2(np.maximum(x, 1))).astype(np.int32), 0)
    n_levels = int(np.log2(c))
    out = np.where(t > s, lvl, np.where(t == s, n_levels, -1)).astype(np.int32)
    return out


def _hgrn_kernel(q_ref, f_ref, i_ref, gate_ref, lbl_ref, gain_ref, lvl_ref, o_ref, st_scr, g_scr, *, layer):
    c = q_ref.shape[1]
    n_levels = c.bit_length() - 1

    @pl.when(pl.program_id(1) == 0)
    def _():
        st_scr[...] = jnp.zeros_like(st_scr)

    a = lbl_ref[...]
    e = jnp.exp(a - jnp.max(a, axis=0, keepdims=True))
    lb = jnp.sum(e[:layer + 1], axis=0, keepdims=True) / jnp.sum(e, axis=0, keepdims=True)

    q = _silu(q_ref[0].astype(F32))
    fg = lb + (1.0 - lb) * _sigmoid(f_ref[0].astype(F32))
    kk = 1.0 - fg
    logf = jnp.log(fg)

    r2 = lax.broadcasted_iota(jnp.int32, (c, c), 0)
    c2 = lax.broadcasted_iota(jnp.int32, (c, c), 1)
    tril = jnp.where(r2 >= c2, 1.0, 0.0).astype(BF16)
    hi = logf.astype(BF16)
    lo = (logf - hi.astype(F32)).astype(BF16)
    g = _dot(tril, hi) + _dot(tril, lo)
    g_scr[...] = g

    row = lax.broadcasted_iota(jnp.int32, (c, HG_DIM), 0)
    exps, rowbs = [], []
    for li in range(n_levels):
        h = 1 << li
        if h >= 4:
            parts = [jnp.broadcast_to(g_scr[pl.ds(p * 2 * h + h - 1, 1), :], (2 * h, HG_WIDTH))
                     for p in range(c // (2 * h))]
            e_l = -jnp.abs(g - jnp.concatenate(parts, axis=0))
            rowb = (row & (2 * h - 1)) >= h
        elif h == 2:
            up = pltpu.roll(logf, 1, 0)
            dn = pltpu.roll(logf, c - 1, 0)
            r4 = lax.broadcasted_iota(jnp.int32, (c, HG_WIDTH), 0) & 3
            e_l = jnp.where(r4 == 0, dn, jnp.where(r4 == 1, 0.0, jnp.where(r4 == 2, logf, up + logf)))
            rowb = (row & 3) >= 2
        else:
            r2w = lax.broadcasted_iota(jnp.int32, (c, HG_WIDTH), 0) & 1
            e_l = jnp.where(r2w == 1, logf, 0.0)
            rowb = (row & 1) == 1
        exps.append(jnp.exp(e_l))
        rowbs.append(rowb)

    eg = jnp.exp(g)
    g_last = g_scr[pl.ds(c - 1, 1), :]
    e_last = jnp.exp(g_last)
    kdec = kk * jnp.exp(g_last - g)
    lvl = lvl_ref[...]
    gate = _silu(gate_ref[0].astype(F32))
    gain = gain_ref[...]

    for hd in range(HG_HEADS):
        sl = slice(hd * HG_DIM, (hd + 1) * HG_DIM)
        qh, kh = q[:, sl], kk[:, sl]
        scores = jnp.where(lvl == n_levels, _dot_nt(qh.astype(BF16), kh.astype(BF16)), 0.0)
        for li in range(n_levels):
            xl = (jnp.where(rowbs[li], qh, kh) * exps[li][:, sl]).astype(BF16)
            scores = jnp.where(lvl == li, _dot_nt(xl, xl), scores)
        v = i_ref[0, :, sl]
        st = st_scr[hd]
        o = _dot(scores.astype(BF16), v)
        o = o + _dot_nt((qh * eg[:, sl]).astype(BF16), st.astype(BF16))
        st_scr[hd] = st * e_last[:, sl] + _dot_tn(v, kdec[:, sl].astype(BF16))
        ms = jnp.mean(o * o, axis=-1, keepdims=True)
        y = o * lax.rsqrt(ms + EPS) * gain[:, sl]
        o_ref[0, :, sl] = (y * gate[:, sl]).astype(o_ref.dtype)


def _hgrn2(proj, lb_logits, gain, layer):
    bsz, seq, _ = proj.shape
    c = HG_CHUNK
    lvl = jnp.asarray(_hgrn_level_table(c))

    def col(k):
        return pl.BlockSpec((1, c, HG_WIDTH), lambda b, n, k=k: (b, n, COLB_HG + k))

    return pl.pallas_call(
        functools.partial(_hgrn_kernel, layer=layer),
        grid=(bsz, seq // c),
        in_specs=[col(0), col(1), col(2), col(3),
                  pl.BlockSpec(lb_logits.shape, lambda b, n: (0, 0)),
                  pl.BlockSpec((1, HG_WIDTH), lambda b, n: (0, 0)),
                  pl.BlockSpec((c, c), lambda b, n: (0, 0))],
        out_specs=pl.BlockSpec((1, c, HG_WIDTH), lambda b, n: (b, n, 0)),
        out_shape=jax.ShapeDtypeStruct((bsz, seq, HG_WIDTH), BF16),
        scratch_shapes=[pltpu.VMEM((HG_HEADS, HG_DIM, HG_DIM), F32),
                        pltpu.VMEM((c, HG_WIDTH), F32)],
        compiler_params=_params("arbitrary", "arbitrary"),
        name="hgrn2",
    )(proj, proj, proj, proj, lb_logits, gain.reshape(1, HG_WIDTH), lvl)


def _attn_kernel(q_ref, kp_ref, kc_ref, vp_ref, vc_ref, o_ref, lse_ref, kbuf, vbuf, o_scr, *, slopes, dil, band):
    blk = ATT_BLOCK
    n_mb = q_ref.shape[2] // blk
    step = pl.program_id(1)
    kbuf[:, 0:blk, :] = kp_ref[0]
    kbuf[:, blk:, :] = kc_ref[0]
    vbuf[:, 0:blk, :] = vp_ref[0]
    vbuf[:, blk:, :] = vc_ref[0]

    qi = lax.broadcasted_iota(jnp.int32, (blk, 2 * blk), 0) + blk
    ki = lax.broadcasted_iota(jnp.int32, (blk, 2 * blk), 1)
    dist = qi - ki
    in_band = (dist >= 0) & (dist <= band)
    distf = dist.astype(F32) * float(dil)

    def body(it, carry):
        mb = it // dil
        r = it - mb * dil
        row0 = pl.multiple_of(mb * blk, blk)
        first_key = jnp.where((step > 0) | (mb > 0), 0, blk)
        valid = in_band & (ki >= first_key)
        tok0 = mb * (blk * dil) + r
        rows = pl.ds(tok0, blk, stride=dil) if dil > 1 else pl.ds(tok0, blk)
        for j in range(ATT_HPG):
            sl = slice(j * ATT_E, (j + 1) * ATT_E)
            q = q_ref[0, r, pl.ds(row0, blk), sl]
            kw = kbuf[r, pl.ds(row0, 2 * blk), sl]
            vw = vbuf[r, pl.ds(row0, 2 * blk), sl]
            s = _dot_nt(q, kw) * (ATT_E ** -0.5)
            s = jnp.where(valid, s - slopes[j] * distf, -jnp.inf)
            m = jnp.max(s, axis=-1, keepdims=True)
            p = jnp.exp(s - m)
            l = jnp.sum(p, axis=-1, keepdims=True)
            o_scr[j, rows, :] = _dot(p.astype(BF16), vw) / l
            lse_ref[0, j, rows, :] = jnp.broadcast_to(m + jnp.log(l), (blk, ATT_E))
        return carry

    lax.fori_loop(0, n_mb * dil, body, 0)
    for j in range(ATT_HPG):
        o_ref[0, :, j * ATT_E:(j + 1) * ATT_E] = o_scr[j].astype(o_ref.dtype)


def _dilated_group(qkv, group):
    bsz, d, lsub, _ = qkv.shape
    seq = d * lsub
    window, dil = ATT_GROUPS[group]
    band = window // dil
    blk = ATT_BLOCK
    assert dil == d and band <= blk and ATT_SUPER % (blk * d) == 0 and seq % ATT_SUPER == 0
    rows = ATT_SUPER // d
    heads = np.arange(group * ATT_HPG, (group + 1) * ATT_HPG) + 1
    slopes = tuple(float(s) for s in np.exp2(-ALIBI_MAX * heads / N_ATT_HEADS))

    def cur(col):
        return pl.BlockSpec((1, d, rows, ATT_GW), lambda b, n: (b, 0, n, col))

    def prev(col):
        return pl.BlockSpec((1, d, blk, ATT_GW), lambda b, n: (b, 0, jnp.maximum(n * (rows // blk) - 1, 0), col))

    return pl.pallas_call(
        functools.partial(_attn_kernel, slopes=slopes, dil=d, band=band),
        grid=(bsz, seq // ATT_SUPER),
        in_specs=[cur(0), prev(1), cur(1), prev(2), cur(2)],
        out_specs=[pl.BlockSpec((1, ATT_SUPER, ATT_GW), lambda b, n: (b, n, 0)),
                   pl.BlockSpec((1, ATT_HPG, ATT_SUPER, ATT_E), lambda b, n: (b, 0, n, 0))],
        out_shape=[jax.ShapeDtypeStruct((bsz, seq, ATT_GW), BF16),
                   jax.ShapeDtypeStruct((bsz, ATT_HPG, seq, ATT_E), F32)],
        scratch_shapes=[pltpu.VMEM((d, rows + blk, ATT_GW), BF16), pltpu.VMEM((d, rows + blk, ATT_GW), BF16),
                        pltpu.VMEM((ATT_HPG, ATT_SUPER, ATT_E), F32)],
        compiler_params=_params("arbitrary", "arbitrary"),
        name=f"dilated_attn_g{group}",
    )(qkv, qkv, qkv, qkv, qkv)


def _mix_kernel(ya_ref, o1_ref, o2_ref, o3_ref, l1_ref, l2_ref, l3_ref, ga_ref, gb_ref, x_ref,
                wa_ref, wb_ref, wo_ref, gt_ref, sc_ref, sh_ref, gn_ref, x1_ref, h2_ref):
    parts = []
    for j in range(ATT_HPG):
        sl = slice(j * ATT_E, (j + 1) * ATT_E)
        l1, l2, l3 = l1_ref[0, j], l2_ref[0, j], l3_ref[0, j]
        m = jnp.maximum(jnp.maximum(l1, l2), l3)
        e1, e2, e3 = jnp.exp(l1 - m), jnp.exp(l2 - m), jnp.exp(l3 - m)
        num = (e1 * o1_ref[0, :, sl].astype(F32) + e2 * o2_ref[0, :, sl].astype(F32)
               + e3 * o3_ref[0, :, sl].astype(F32))
        parts.append(num / (e1 + e2 + e3))
    yb = jnp.concatenate(parts, axis=1)
    za = _dot(ya_ref[0], wa_ref[...])
    zb = _dot(yb.astype(BF16), wb_ref[...])
    merged = _sigmoid(ga_ref[0].astype(F32)) * za + _sigmoid(gb_ref[0].astype(F32)) * zb
    x1 = x_ref[0] + (1.0 + gt_ref[0]) * _dot(merged.astype(BF16), wo_ref[...])
    x1_ref[0] = x1
    ms = jnp.mean(x1 * x1, axis=-1, keepdims=True)
    y = x1 * lax.rsqrt(ms + EPS) * gn_ref[...]
    h2_ref[0] = (y * (1.0 + sc_ref[0]) + sh_ref[0]).astype(h2_ref.dtype)


def _mix(ya, outs, lses, main, x, wa, wb, wo, gt, sc, sh, gn):
    bsz, seq, d = x.shape
    tm = 512

    def tok(width, colb=0):
        return pl.BlockSpec((1, tm, width), lambda b, i, colb=colb: (b, i, colb))

    def full(shape):
        return pl.BlockSpec(shape, lambda b, i: (0,) * len(shape), pipeline_mode=pl.Buffered(1))

    def per_batch():
        return pl.BlockSpec((1, 1, d), lambda b, i: (b, 0, 0))

    lse_spec = pl.BlockSpec((1, ATT_HPG, tm, ATT_E), lambda b, i: (b, 0, i, 0))
    return pl.pallas_call(
        _mix_kernel,
        grid=(bsz, seq // tm),
        in_specs=[tok(HG_WIDTH)] + [tok(ATT_GW)] * 3 + [lse_spec] * 3 + [tok(d, COLB_GA // 2), tok(d, COLB_GB // 2), tok(d),
                  full(wa.shape), full(wb.shape), full(wo.shape),
                  per_batch(), per_batch(), per_batch(), full((1, d))],
        out_specs=[tok(d), tok(d)],
        out_shape=[jax.ShapeDtypeStruct((bsz, seq, d), F32), jax.ShapeDtypeStruct((bsz, seq, d), BF16)],
        compiler_params=_params("arbitrary", "arbitrary"),
        name="mix",
    )(ya, *outs, *lses, main, main, x, wa, wb, wo, gt, sc, sh, gn.reshape(1, d))


def _ffn_kernel(h_ref, x_ref, wu_ref, wd_ref, cw_ref, cb_ref, gt_ref, gf_ref, o_ref, carry_scr, acc_scr, *,
                final_norm):
    tm = h_ref.shape[1]
    fc = FFN_CHUNK

    @pl.when(pl.program_id(1) == 0)
    def _():
        carry_scr[...] = jnp.zeros_like(carry_scr)

    h = h_ref[0]
    row = lax.broadcasted_iota(jnp.int32, (tm, fc), 0)
    for ci in range(D_FF // fc):
        cs = slice(ci * fc, (ci + 1) * fc)
        ug = _dot(h, wu_ref[:, cs])
        uv = _dot(h, wu_ref[:, D_FF + ci * fc:D_FF + (ci + 1) * fc])
        prev2 = jnp.broadcast_to(carry_scr[6:7, cs], (tm, fc))
        prev1 = jnp.broadcast_to(carry_scr[7:8, cs], (tm, fc))
        r1 = pltpu.roll(ug, 1, 0)
        r2 = pltpu.roll(ug, 2, 0)
        u1 = jnp.where(row == 0, prev1, r1)
        u2 = jnp.where(row == 0, prev2, jnp.where(row == 1, prev1, r2))
        carry_scr[:, cs] = ug[tm - 8:tm, :]
        y = cb_ref[:, cs] + u2 * cw_ref[0:1, cs] + u1 * cw_ref[1:2, cs] + ug * cw_ref[2:3, cs]
        gel = 0.5 * y * (1.0 + jnp.tanh(0.7978845608028654 * (y + 0.044715 * (y * y * y))))
        part = _dot((gel * uv).astype(BF16), wd_ref[cs, :])
        if ci == 0:
            acc_scr[...] = part
        else:
            acc_scr[...] += part
    x2 = x_ref[0] + (1.0 + gt_ref[0]) * acc_scr[...]
    if final_norm:
        ms = jnp.mean(x2 * x2, axis=-1, keepdims=True)
        x2 = x2 * lax.rsqrt(ms + EPS) * gf_ref[...]
    o_ref[0] = x2


def _ffn(h2, x1, wu, wd, cw, cb, gt, gf, final_norm):
    bsz, seq, d = x1.shape
    tm = 512

    def tok():
        return pl.BlockSpec((1, tm, d), lambda b, i: (b, i, 0))

    def full(shape):
        return pl.BlockSpec(shape, lambda b, i: (0,) * len(shape), pipeline_mode=pl.Buffered(1))

    return pl.pallas_call(
        functools.partial(_ffn_kernel, final_norm=final_norm),
        grid=(bsz, seq // tm),
        in_specs=[tok(), tok(), full(wu.shape), full(wd.shape), full(cw.shape), full((1, D_FF)),
                  pl.BlockSpec((1, 1, d), lambda b, i: (b, 0, 0)), full((1, d))],
        out_specs=tok(),
        out_shape=jax.ShapeDtypeStruct((bsz, seq, d), F32),
        scratch_shapes=[pltpu.VMEM((8, D_FF), F32), pltpu.VMEM((tm, d), F32)],
        compiler_params=_params("arbitrary", "arbitrary"),
        name="ffn",
    )(h2, x1, wu, wd, cw, cb.reshape(1, D_FF), gt, gf.reshape(1, d))


def kernel(x, c, w_ada, b_ada, g_norm_mix, w_in, lb_logits, g_hg_norm, w_branch_a, w_branch_b, w_out, g_norm_ffn,
           w_up, conv_w, conv_b, w_down, g_final):
    depth = w_ada.shape[0]
    bsz = x.shape[0]
    n_hg = 4 * HG_WIDTH
    for l in range(depth):
        mod = _modulation(c, w_ada[l], b_ada[l]).reshape(bsz, N_MOD, 1, D_MODEL)
        sh1, sc1, gt1, sh2, sc2, gt2 = [mod[:, k] for k in range(N_MOD)]
        w = w_in[l]
        n_grp = len(ATT_GROUPS)
        o_q = n_hg
        o_g = n_hg + 3 * n_grp * ATT_GW
        cols = [w[:, o_g:], w[:, :n_hg]]
        for g in range(n_grp):
            cols += [w[:, o_q + (t * n_grp + g) * ATT_GW: o_q + (t * n_grp + g + 1) * ATT_GW] for t in range(3)]
        w_perm = jnp.concatenate(cols, axis=1).astype(BF16)

        main, *qkvs = _in_projection(x, g_norm_mix[l], sc1, sh1, w_perm)
        ya = _hgrn2(main, lb_logits, g_hg_norm[l].reshape(HG_WIDTH), l)
        groups = [_dilated_group(qkvs[g], g) for g in range(n_grp)]
        x1, h2 = _mix(ya, [o for o, _ in groups], [s for _, s in groups], main, x,
                      w_branch_a[l].astype(BF16), w_branch_b[l].astype(BF16), w_out[l].astype(BF16),
                      gt1, sc2, sh2, g_norm_ffn[l])
        x = _ffn(h2, x1, w_up[l].astype(BF16), w_down[l].astype(BF16), conv_w[l], conv_b[l], gt2, g_final,
                 final_norm=(l == depth - 1))
    return x
```

```python
import functools

import numpy as np
import jax
import jax.numpy as jnp
from jax import lax
from jax.experimental import pallas as pl
from jax.experimental.pallas import tpu as pltpu

F32 = jnp.float32
BF16 = jnp.bfloat16

D_MODEL = 1024
HG_HEADS = 4
HG_DIM = 128
HG_WIDTH = HG_HEADS * HG_DIM
ATT_GROUPS = ((128, 1), (512, 4), (2048, 16))
ATT_HPG = 4
ATT_E = 128
N_ATT_HEADS = ATT_HPG * len(ATT_GROUPS)
ATT_GW = ATT_HPG * ATT_E
ATT_BLOCK = 128
ALIBI_MAX = 8.0
D_FF = 2816
N_MOD = 6
EPS = 1e-6

COLB_GA = 0
COLB_GB = 2
COLB_HG = 4
N_MAIN = 8 * 512
N_QKV = 3 * ATT_GW
ATT_SUPER = 2048
ATT_UNROLL = 4
LANES = 128

VMEM_LIMIT_BYTES = 56 * 1024 * 1024

HG_CHUNK = 128
FFN_CHUNK = 256


def _sigmoid(x):
    return 1.0 / (1.0 + jnp.exp(-x))


def _gate_sigmoid(x):
    return 0.5 * jnp.tanh(0.5 * x) + 0.5


def _silu(x):
    return x * _gate_sigmoid(x)


def _dot(a, b):
    return jnp.dot(a, b, preferred_element_type=F32)


def _dot_nt(a, b):
    return lax.dot_general(a, b, (((1,), (1,)), ((), ())), preferred_element_type=F32)


def _dot_tn(a, b):
    return lax.dot_general(a, b, (((0,), (0,)), ((), ())), preferred_element_type=F32)


def _params(*sem):
    return pltpu.CompilerParams(dimension_semantics=sem, vmem_limit_bytes=VMEM_LIMIT_BYTES)


def _mod_kernel(c_ref, w_ref, b_ref, o_ref):
    s = _silu(c_ref[...]).astype(BF16)
    o_ref[...] = _dot(s, w_ref[...].astype(BF16)) + b_ref[...]


def _modulation(c, w, b):
    bsz, d = c.shape
    n = w.shape[1]
    tn = 1024
    return pl.pallas_call(
        _mod_kernel,
        grid=(n // tn,),
        in_specs=[pl.BlockSpec((bsz, d), lambda j: (0, 0)),
                  pl.BlockSpec((d, tn), lambda j: (0, j)),
                  pl.BlockSpec((1, tn), lambda j: (0, j))],
        out_specs=pl.BlockSpec((bsz, tn), lambda j: (0, j)),
        out_shape=jax.ShapeDtypeStruct((bsz, n), F32),
        compiler_params=_params("arbitrary"),
        name="modulation",
    )(c, w, b.reshape(1, n))


def _inproj_kernel(x_ref, g_ref, sc_ref, sh_ref, w_ref, main_ref, a0_ref, a1_ref, a2_ref, slab_scr, hp1_scr, hp2_scr):
    tm, d_model = x_ref.shape[1], x_ref.shape[2]
    x = x_ref[0]
    ms = jnp.mean(x * x, axis=-1, keepdims=True)
    scale = g_ref[...] * (1.0 + sc_ref[0])
    h = x * lax.rsqrt(ms + EPS) * scale + sh_ref[0]
    hb = h.astype(BF16)
    for c0 in range(0, N_MAIN, 1024):
        main_ref[0, :, c0:c0 + 1024] = _dot(hb, w_ref[:, c0:c0 + 1024]).astype(main_ref.dtype)
    a0_ref[0, 0] = _dot(hb, w_ref[:, N_MAIN:N_MAIN + N_QKV]).astype(a0_ref.dtype)

    n_slabs = d_model // LANES
    for s in range(n_slabs):
        slab_scr[s] = h[:, s * LANES:(s + 1) * LANES]
    for gi, (out_ref, hp_scr) in enumerate(((a1_ref, hp1_scr), (a2_ref, hp2_scr)), start=1):
        dil = ATT_GROUPS[gi][1]
        rows = tm // dil
        for r in range(dil):
            piece = jnp.concatenate([slab_scr[s, pl.ds(r, rows, stride=dil), :] for s in range(n_slabs)], axis=1)
            hp_scr[r * rows:(r + 1) * rows, :] = piece.astype(BF16)
        c0 = N_MAIN + gi * N_QKV
        res = _dot(hp_scr[...], w_ref[:, c0:c0 + N_QKV]).astype(out_ref.dtype)
        for r in range(dil):
            out_ref[0, r] = res[r * rows:(r + 1) * rows, :]


def _in_projection(x, gain, sc, sh, w):
    bsz, seq, d = x.shape
    tm = 512
    dils = [dil for _, dil in ATT_GROUPS]
    assert dils[0] == 1 and all(tm % (16 * dil) == 0 for dil in dils)

    def full(shape):
        return pl.BlockSpec(shape, lambda b, i: (0,) * len(shape), pipeline_mode=pl.Buffered(1))

    out_shape = [jax.ShapeDtypeStruct((bsz, seq, N_MAIN), BF16)]
    out_specs = [pl.BlockSpec((1, tm, N_MAIN), lambda b, i: (b, i, 0))]
    for dil in dils:
        out_shape.append(jax.ShapeDtypeStruct((bsz, dil, seq // dil, N_QKV), BF16))
        out_specs.append(pl.BlockSpec((1, dil, tm // dil, N_QKV), lambda b, i: (b, 0, i, 0)))
    return pl.pallas_call(
        _inproj_kernel,
        grid=(bsz, seq // tm),
        in_specs=[pl.BlockSpec((1, tm, d), lambda b, i: (b, i, 0)),
                  full((1, d)),
                  pl.BlockSpec((1, 1, d), lambda b, i: (b, 0, 0)),
                  pl.BlockSpec((1, 1, d), lambda b, i: (b, 0, 0)),
                  full(w.shape)],
        out_specs=out_specs,
        out_shape=out_shape,
        scratch_shapes=[pltpu.VMEM((d // LANES, tm, LANES), F32),
                        pltpu.VMEM((tm, d), BF16), pltpu.VMEM((tm, d), BF16)],
        compiler_params=_params("arbitrary", "arbitrary"),
        name="in_projection",
    )(x, gain.reshape(1, d), sc, sh, w)


def _hgrn_level_table(c):
    t = np.arange(c)[:, None]
    s = np.arange(c)[None, :]
    x = t ^ s
    lvl = np.where(x > 0, np.floor(np.log2(np.maximum(x, 1))).astype(np.int32), 0)
    n_levels = int(np.log2(c))
    out = np.where(t > s, lvl, np.where(t == s, n_levels, -1)).astype(np.int32)
    return out


def _hgrn_kernel(q_ref, f_ref, i_ref, gate_ref, lbl_ref, gain_ref, lvl_ref, o_ref, st_scr, g_scr, *, layer):
    c = q_ref.shape[1]
    n_levels = c.bit_length() - 1

    @pl.when(pl.program_id(1) == 0)
    def _():
        st_scr[...] = jnp.zeros_like(st_scr)

    a = lbl_ref[...]
    e = jnp.exp(a - jnp.max(a, axis=0, keepdims=True))
    lb = jnp.sum(e[:layer + 1], axis=0, keepdims=True) / jnp.sum(e, axis=0, keepdims=True)

    q = _silu(q_ref[0].astype(F32))
    fg = lb + (1.0 - lb) * _sigmoid(f_ref[0].astype(F32))
    kk = 1.0 - fg
    logf = jnp.log(fg)

    r2 = lax.broadcasted_iota(jnp.int32, (c, c), 0)
    c2 = lax.broadcasted_iota(jnp.int32, (c, c), 1)
    tril = jnp.where(r2 >= c2, 1.0, 0.0).astype(BF16)
    hi = logf.astype(BF16)
    lo = (logf - hi.astype(F32)).astype(BF16)
    g = _dot(tril, hi) + _dot(tril, lo)
    g_scr[...] = g

    row = lax.broadcasted_iota(jnp.int32, (c, HG_DIM), 0)
    exps, rowbs = [], []
    for li in range(n_levels):
        h = 1 << li
        if h >= 4:
            parts = [jnp.broadcast_to(g_scr[pl.ds(p * 2 * h + h - 1, 1), :], (2 * h, HG_WIDTH))
                     for p in range(c // (2 * h))]
            e_l = -jnp.abs(g - jnp.concatenate(parts, axis=0))
            rowb = (row & (2 * h - 1)) >= h
        elif h == 2:
            up = pltpu.roll(logf, 1, 0)
            dn = pltpu.roll(logf, c - 1, 0)
            r4 = lax.broadcasted_iota(jnp.int32, (c, HG_WIDTH), 0) & 3
            e_l = jnp.where(r4 == 0, dn, jnp.where(r4 == 1, 0.0, jnp.where(r4 == 2, logf, up + logf)))
            rowb = (row & 3) >= 2
        else:
            r2w = lax.broadcasted_iota(jnp.int32, (c, HG_WIDTH), 0) & 1
            e_l = jnp.where(r2w == 1, logf, 0.0)
            rowb = (row & 1) == 1
        exps.append(jnp.exp(e_l))
        rowbs.append(rowb)

    eg = jnp.exp(g)
    g_last = g_scr[pl.ds(c - 1, 1), :]
    e_last = jnp.exp(g_last)
    kdec = kk * jnp.exp(g_last - g)
    lvl = lvl_ref[...]
    gate = _silu(gate_ref[0].astype(F32))
    gain = gain_ref[...]

    for hd in range(HG_HEADS):
        sl = slice(hd * HG_DIM, (hd + 1) * HG_DIM)
        qh, kh = q[:, sl], kk[:, sl]
        scores = jnp.where(lvl == n_levels, _dot_nt(qh.astype(BF16), kh.astype(BF16)), 0.0)
        for li in range(n_levels):
            xl = (jnp.where(rowbs[li], qh, kh) * exps[li][:, sl]).astype(BF16)
            scores = jnp.where(lvl == li, _dot_nt(xl, xl), scores)
        v = i_ref[0, :, sl]
        st = st_scr[hd]
        o = _dot(scores.astype(BF16), v)
        o = o + _dot_nt((qh * eg[:, sl]).astype(BF16), st.astype(BF16))
        st_scr[hd] = st * e_last[:, sl] + _dot_tn(v, kdec[:, sl].astype(BF16))
        ms = jnp.mean(o * o, axis=-1, keepdims=True)
        y = o * lax.rsqrt(ms + EPS) * gain[:, sl]
        o_ref[0, :, sl] = (y * gate[:, sl]).astype(o_ref.dtype)


def _hgrn2(proj, lb_logits, gain, layer):
    bsz, seq, _ = proj.shape
    c = HG_CHUNK
    lvl = jnp.asarray(_hgrn_level_table(c))

    def col(k):
        return pl.BlockSpec((1, c, HG_WIDTH), lambda b, n, k=k: (b, n, COLB_HG + k))

    return pl.pallas_call(
        functools.partial(_hgrn_kernel, layer=layer),
        grid=(bsz, seq // c),
        in_specs=[col(0), col(1), col(2), col(3),
                  pl.BlockSpec(lb_logits.shape, lambda b, n: (0, 0)),
                  pl.BlockSpec((1, HG_WIDTH), lambda b, n: (0, 0)),
                  pl.BlockSpec((c, c), lambda b, n: (0, 0))],
        out_specs=pl.BlockSpec((1, c, HG_WIDTH), lambda b, n: (b, n, 0)),
        out_shape=jax.ShapeDtypeStruct((bsz, seq, HG_WIDTH), BF16),
        scratch_shapes=[pltpu.VMEM((HG_HEADS, HG_DIM, HG_DIM), F32),
                        pltpu.VMEM((c, HG_WIDTH), F32)],
        compiler_params=_params("arbitrary", "arbitrary"),
        name="hgrn2",
    )(proj, proj, proj, proj, lb_logits, gain.reshape(1, HG_WIDTH), lvl)


def _attn_kernel(q_ref, kp_ref, kc_ref, vp_ref, vc_ref, o_ref, lse_ref, kbuf, vbuf, o_scr, *, slopes, dil, band):
    blk = ATT_BLOCK
    n_mb = q_ref.shape[2] // blk
    step = pl.program_id(1)
    kbuf[:, 0:blk, :] = kp_ref[0]
    kbuf[:, blk:, :] = kc_ref[0]
    vbuf[:, 0:blk, :] = vp_ref[0]
    vbuf[:, blk:, :] = vc_ref[0]

    qi = lax.broadcasted_iota(jnp.int32, (blk, 2 * blk), 0) + blk
    ki = lax.broadcasted_iota(jnp.int32, (blk, 2 * blk), 1)
    dist = qi - ki
    in_band = (dist >= 0) & (dist <= band)
    distf = dist.astype(F32) * float(dil)

    def body(it, carry):
        mb = it // dil
        r = it - mb * dil
        row0 = pl.multiple_of(mb * blk, blk)
        first_key = jnp.where((step > 0) | (mb > 0), 0, blk)
        valid = in_band & (ki >= first_key)
        tok0 = mb * (blk * dil) + r
        rows = pl.ds(tok0, blk, stride=dil) if dil > 1 else pl.ds(tok0, blk)
        ones = jnp.ones((2 * blk, ATT_E), BF16)
        for j in range(ATT_HPG):
            sl = slice(j * ATT_E, (j + 1) * ATT_E)
            q = q_ref[0, r, pl.ds(row0, blk), sl]
            kw = kbuf[r, pl.ds(row0, 2 * blk), sl]
            vw = jnp.concatenate([vbuf[r, pl.ds(row0, 2 * blk), sl], ones], axis=1)
            s = _dot_nt(q, kw) * (ATT_E ** -0.5)
            s = jnp.where(valid, s - slopes[j] * distf, -jnp.inf)
            m = jnp.max(jnp.maximum(s[:, :blk], s[:, blk:]), axis=-1, keepdims=True)
            p = jnp.exp(s - m)
            ol = _dot(p.astype(BF16), vw)
            l = ol[:, ATT_E:]
            o_scr[j, rows, :] = ol[:, :ATT_E] / l
            lse_ref[0, j, rows, :] = m + jnp.log(l)
        return carry

    lax.fori_loop(0, n_mb * dil, body, 0, unroll=ATT_UNROLL)
    for j in range(ATT_HPG):
        o_ref[0, :, j * ATT_E:(j + 1) * ATT_E] = o_scr[j].astype(o_ref.dtype)


def _dilated_group(qkv, group):
    bsz, d, lsub, _ = qkv.shape
    seq = d * lsub
    window, dil = ATT_GROUPS[group]
    band = window // dil
    blk = ATT_BLOCK
    assert dil == d and band <= blk and ATT_SUPER % (blk * d) == 0 and seq % ATT_SUPER == 0
    rows = ATT_SUPER // d
    heads = np.arange(group * ATT_HPG, (group + 1) * ATT_HPG) + 1
    slopes = tuple(float(s) for s in np.exp2(-ALIBI_MAX * heads / N_ATT_HEADS))

    def cur(col):
        return pl.BlockSpec((1, d, rows, ATT_GW), lambda b, n: (b, 0, n, col))

    def prev(col):
        return pl.BlockSpec((1, d, blk, ATT_GW), lambda b, n: (b, 0, jnp.maximum(n * (rows // blk) - 1, 0), col))

    return pl.pallas_call(
        functools.partial(_attn_kernel, slopes=slopes, dil=d, band=band),
        grid=(bsz, seq // ATT_SUPER),
        in_specs=[cur(0), prev(1), cur(1), prev(2), cur(2)],
        out_specs=[pl.BlockSpec((1, ATT_SUPER, ATT_GW), lambda b, n: (b, n, 0)),
                   pl.BlockSpec((1, ATT_HPG, ATT_SUPER, ATT_E), lambda b, n: (b, 0, n, 0))],
        out_shape=[jax.ShapeDtypeStruct((bsz, seq, ATT_GW), BF16),
                   jax.ShapeDtypeStruct((bsz, ATT_HPG, seq, ATT_E), F32)],
        scratch_shapes=[pltpu.VMEM((d, rows + blk, ATT_GW), BF16), pltpu.VMEM((d, rows + blk, ATT_GW), BF16),
                        pltpu.VMEM((ATT_HPG, ATT_SUPER, ATT_E), F32)],
        compiler_params=_params("arbitrary", "arbitrary"),
        name=f"dilated_attn_g{group}",
    )(qkv, qkv, qkv, qkv, qkv)


def _mix_kernel(ya_ref, o1_ref, o2_ref, o3_ref, l1_ref, l2_ref, l3_ref, ga_ref, gb_ref, x_ref,
                wa_ref, wb_ref, wo_ref, gt_ref, sc_ref, sh_ref, gn_ref, x1_ref, h2_ref):
    parts = []
    for j in range(ATT_HPG):
        sl = slice(j * ATT_E, (j + 1) * ATT_E)
        l1, l2, l3 = l1_ref[0, j], l2_ref[0, j], l3_ref[0, j]
        m = jnp.maximum(jnp.maximum(l1, l2), l3)
        e1, e2, e3 = jnp.exp(l1 - m), jnp.exp(l2 - m), jnp.exp(l3 - m)
        num = (e1 * o1_ref[0, :, sl].astype(F32) + e2 * o2_ref[0, :, sl].astype(F32)
               + e3 * o3_ref[0, :, sl].astype(F32))
        parts.append(num / (e1 + e2 + e3))
    yb = jnp.concatenate(parts, axis=1)
    za = _dot(ya_ref[0], wa_ref[...])
    zb = _dot(yb.astype(BF16), wb_ref[...])
    merged = _gate_sigmoid(ga_ref[0].astype(F32)) * za + _gate_sigmoid(gb_ref[0].astype(F32)) * zb
    x1 = x_ref[0] + (1.0 + gt_ref[0]) * _dot(merged.astype(BF16), wo_ref[...])
    x1_ref[0] = x1
    ms = jnp.mean(x1 * x1, axis=-1, keepdims=True)
    scale = gn_ref[...] * (1.0 + sc_ref[0])
    h2_ref[0] = (x1 * lax.rsqrt(ms + EPS) * scale + sh_ref[0]).astype(h2_ref.dtype)


def _mix(ya, outs, lses, main, x, wa, wb, wo, gt, sc, sh, gn):
    bsz, seq, d = x.shape
    tm = 512

    def tok(width, colb=0):
        return pl.BlockSpec((1, tm, width), lambda b, i, colb=colb: (b, i, colb))

    def full(shape):
        return pl.BlockSpec(shape, lambda b, i: (0,) * len(shape), pipeline_mode=pl.Buffered(1))

    def per_batch():
        return pl.BlockSpec((1, 1, d), lambda b, i: (b, 0, 0))

    lse_spec = pl.BlockSpec((1, ATT_HPG, tm, ATT_E), lambda b, i: (b, 0, i, 0))
    return pl.pallas_call(
        _mix_kernel,
        grid=(bsz, seq // tm),
        in_specs=[tok(HG_WIDTH)] + [tok(ATT_GW)] * 3 + [lse_spec] * 3 + [tok(d, COLB_GA // 2), tok(d, COLB_GB // 2), tok(d),
                  full(wa.shape), full(wb.shape), full(wo.shape),
                  per_batch(), per_batch(), per_batch(), full((1, d))],
        out_specs=[tok(d), tok(d)],
        out_shape=[jax.ShapeDtypeStruct((bsz, seq, d), F32), jax.ShapeDtypeStruct((bsz, seq, d), BF16)],
        compiler_params=_params("arbitrary", "arbitrary"),
        name="mix",
    )(ya, *outs, *lses, main, main, x, wa, wb, wo, gt, sc, sh, gn.reshape(1, d))


def _ffn_kernel(h_ref, x_ref, wu_ref, wd_ref, cw_ref, cb_ref, gt_ref, gf_ref, o_ref, carry_scr, ug_scr, uv_scr,
                act_scr, *, final_norm):
    tm = h_ref.shape[1]
    fc = FFN_CHUNK

    @pl.when(pl.program_id(1) == 0)
    def _():
        carry_scr[...] = jnp.zeros_like(carry_scr)

    h = h_ref[0]
    row = lax.broadcasted_iota(jnp.int32, (tm, fc), 0)
    n_chunks = D_FF // fc

    def up_project(ci):
        ug_scr[ci % 2] = _dot(h, wu_ref[:, ci * fc:(ci + 1) * fc])
        uv_scr[ci % 2] = _dot(h, wu_ref[:, D_FF + ci * fc:D_FF + (ci + 1) * fc])

    up_project(0)
    for ci in range(n_chunks):
        if ci + 1 < n_chunks:
            up_project(ci + 1)
        cs = slice(ci * fc, (ci + 1) * fc)
        ug = ug_scr[ci % 2]
        prev2 = jnp.broadcast_to(carry_scr[6:7, cs], (tm, fc))
        prev1 = jnp.broadcast_to(carry_scr[7:8, cs], (tm, fc))
        r1 = pltpu.roll(ug, 1, 0)
        r2 = pltpu.roll(ug, 2, 0)
        u1 = jnp.where(row == 0, prev1, r1)
        u2 = jnp.where(row == 0, prev2, jnp.where(row == 1, prev1, r2))
        carry_scr[:, cs] = ug[tm - 8:tm, :]
        y = cb_ref[:, cs] + u2 * cw_ref[0:1, cs] + u1 * cw_ref[1:2, cs] + ug * cw_ref[2:3, cs]
        gel = 0.5 * y * (1.0 + jnp.tanh(0.7978845608028654 * (y + 0.044715 * (y * y * y))))
        act_scr[:, cs] = (gel * uv_scr[ci % 2]).astype(BF16)
    x2 = x_ref[0] + (1.0 + gt_ref[0]) * _dot(act_scr[...], wd_ref[...])
    if final_norm:
        ms = jnp.mean(x2 * x2, axis=-1, keepdims=True)
        x2 = x2 * lax.rsqrt(ms + EPS) * gf_ref[...]
    o_ref[0] = x2


def _ffn(h2, x1, wu, wd, cw, cb, gt, gf, final_norm):
    bsz, seq, d = x1.shape
    tm = 512

    def tok():
        return pl.BlockSpec((1, tm, d), lambda b, i: (b, i, 0))

    def full(shape):
        return pl.BlockSpec(shape, lambda b, i: (0,) * len(shape), pipeline_mode=pl.Buffered(1))

    return pl.pallas_call(
        functools.partial(_ffn_kernel, final_norm=final_norm),
        grid=(bsz, seq // tm),
        in_specs=[tok(), tok(), full(wu.shape), full(wd.shape), full(cw.shape), full((1, D_FF)),
                  pl.BlockSpec((1, 1, d), lambda b, i: (b, 0, 0)), full((1, d))],
        out_specs=tok(),
        out_shape=jax.ShapeDtypeStruct((bsz, seq, d), F32),
        scratch_shapes=[pltpu.VMEM((8, D_FF), F32), pltpu.VMEM((2, tm, FFN_CHUNK), F32),
                        pltpu.VMEM((2, tm, FFN_CHUNK), F32), pltpu.VMEM((tm, D_FF), BF16)],
        compiler_params=_params("arbitrary", "arbitrary"),
        name="ffn",
    )(h2, x1, wu, wd, cw, cb.reshape(1, D_FF), gt, gf.reshape(1, d))


def kernel(x, c, w_ada, b_ada, g_norm_mix, w_in, lb_logits, g_hg_norm, w_branch_a, w_branch_b, w_out, g_norm_ffn,
           w_up, conv_w, conv_b, w_down, g_final):
    depth = w_ada.shape[0]
    bsz = x.shape[0]
    n_hg = 4 * HG_WIDTH
    for l in range(depth):
        mod = _modulation(c, w_ada[l], b_ada[l]).reshape(bsz, N_MOD, 1, D_MODEL)
        sh1, sc1, gt1, sh2, sc2, gt2 = [mod[:, k] for k in range(N_MOD)]
        w = w_in[l]
        n_grp = len(ATT_GROUPS)
        o_q = n_hg
        o_g = n_hg + 3 * n_grp * ATT_GW
        cols = [w[:, o_g:], w[:, :n_hg]]
        for g in range(n_grp):
            cols += [w[:, o_q + (t * n_grp + g) * ATT_GW: o_q + (t * n_grp + g + 1) * ATT_GW] for t in range(3)]
        w_perm = jnp.concatenate(cols, axis=1).astype(BF16)

        main, *qkvs = _in_projection(x, g_norm_mix[l], sc1, sh1, w_perm)
        ya = _hgrn2(main, lb_logits, g_hg_norm[l].reshape(HG_WIDTH), l)
        groups = [_dilated_group(qkvs[g], g) for g in range(n_grp)]
        x1, h2 = _mix(ya, [o for o, _ in groups], [s for _, s in groups], main, x,
                      w_branch_a[l].astype(BF16), w_branch_b[l].astype(BF16), w_out[l].astype(BF16),
                      gt1, sc2, sh2, g_norm_ffn[l])
        x = _ffn(h2, x1, w_up[l].astype(BF16), w_down[l].astype(BF16), conv_w[l], conv_b[l], gt2, g_final,
                 final_norm=(l == depth - 1))
    return x
```

```python
import functools

import numpy as np
import jax
import jax.numpy as jnp
from jax import lax
from jax.experimental import pallas as pl
from jax.experimental.pallas import tpu as pltpu

F32 = jnp.float32
BF16 = jnp.bfloat16

D_MODEL = 1024
HG_HEADS = 4
HG_DIM = 128
HG_WIDTH = HG_HEADS * HG_DIM
ATT_GROUPS = ((128, 1), (512, 4), (2048, 16))
ATT_HPG = 4
ATT_E = 128
N_ATT_HEADS = ATT_HPG * len(ATT_GROUPS)
ATT_GW = ATT_HPG * ATT_E
ATT_BLOCK = 128
ALIBI_MAX = 8.0
D_FF = 2816
N_MOD = 6
EPS = 1e-6

N_GATES = 2 * D_MODEL
N_QKV = 3 * ATT_GW
ATT_SUPER = 2048
ATT_UNROLL = 4
LANES = 128

VMEM_LIMIT_BYTES = 56 * 1024 * 1024

HG_CHUNK = 128
FFN_CHUNK = 256


def _sigmoid(x):
    return 1.0 / (1.0 + jnp.exp(-x))


def _gate_sigmoid(x):
    return 0.5 * jnp.tanh(0.5 * x) + 0.5


def _silu(x):
    return x * _gate_sigmoid(x)


def _dot(a, b):
    return jnp.dot(a, b, preferred_element_type=F32)


def _dot_nt(a, b):
    return lax.dot_general(a, b, (((1,), (1,)), ((), ())), preferred_element_type=F32)


def _dot_tn(a, b):
    return lax.dot_general(a, b, (((0,), (0,)), ((), ())), preferred_element_type=F32)


def _params(*sem):
    return pltpu.CompilerParams(dimension_semantics=sem, vmem_limit_bytes=VMEM_LIMIT_BYTES)


def _mod_kernel(c_ref, w_ref, b_ref, o_ref):
    s = _silu(c_ref[...]).astype(BF16)
    o_ref[...] = _dot(s, w_ref[...].astype(BF16)) + b_ref[...]


def _modulation(c, w, b):
    bsz, d = c.shape
    n = w.shape[1]
    tn = 1024
    return pl.pallas_call(
        _mod_kernel,
        grid=(n // tn,),
        in_specs=[pl.BlockSpec((bsz, d), lambda j: (0, 0)),
                  pl.BlockSpec((d, tn), lambda j: (0, j)),
                  pl.BlockSpec((1, tn), lambda j: (0, j))],
        out_specs=pl.BlockSpec((bsz, tn), lambda j: (0, j)),
        out_shape=jax.ShapeDtypeStruct((bsz, n), F32),
        compiler_params=_params("arbitrary"),
        name="modulation",
    )(c, w, b.reshape(1, n))


def _inproj_kernel(x_ref, g_ref, sc_ref, sh_ref, w_ref, lbl_ref, hgain_ref, lvl_ref,
                   gates_ref, a0_ref, a1_ref, a2_ref, ya_ref,
                   hg_scr, slab_scr, hp1_scr, hp2_scr, st_scr, g_scr, *, layer):
    tm, d_model = x_ref.shape[1], x_ref.shape[2]

    @pl.when(pl.program_id(1) == 0)
    def _():
        st_scr[...] = jnp.zeros_like(st_scr)

    x = x_ref[0]
    ms = jnp.mean(x * x, axis=-1, keepdims=True)
    scale = g_ref[...] * (1.0 + sc_ref[0])
    h = x * lax.rsqrt(ms + EPS) * scale + sh_ref[0]
    hb = h.astype(BF16)
    n_slabs = d_model // LANES
    for s in range(n_slabs):
        slab_scr[s] = h[:, s * LANES:(s + 1) * LANES]

    n_hg = 4 * HG_WIDTH
    for c0 in range(0, n_hg, 1024):
        hg_scr[:, c0:c0 + 1024] = _dot(hb, w_ref[:, N_GATES + c0:N_GATES + c0 + 1024]).astype(BF16)

    def gate_columns(c0):
        gates_ref[0, :, c0:c0 + 1024] = _dot(hb, w_ref[:, c0:c0 + 1024]).astype(gates_ref.dtype)

    def group0():
        c0 = N_GATES + n_hg
        a0_ref[0, 0] = _dot(hb, w_ref[:, c0:c0 + N_QKV]).astype(a0_ref.dtype)

    def dilated_group(gi, out_ref, hp_scr):
        dil = ATT_GROUPS[gi][1]
        rows = tm // dil
        for r in range(dil):
            piece = jnp.concatenate([slab_scr[s, pl.ds(r, rows, stride=dil), :] for s in range(n_slabs)], axis=1)
            hp_scr[r * rows:(r + 1) * rows, :] = piece.astype(BF16)
        c0 = N_GATES + n_hg + gi * N_QKV
        res = _dot(hp_scr[...], w_ref[:, c0:c0 + N_QKV]).astype(out_ref.dtype)
        for r in range(dil):
            out_ref[0, r] = res[r * rows:(r + 1) * rows, :]

    matmul_pieces = [lambda: gate_columns(0), lambda: gate_columns(1024), group0,
                     lambda: dilated_group(1, a1_ref, hp1_scr), lambda: dilated_group(2, a2_ref, hp2_scr)]
    lb = _hgrn_lower_bound(lbl_ref, layer)
    gain = hgain_ref[...]
    lvl = lvl_ref[...]
    c = HG_CHUNK
    n_chunks = tm // c
    assert n_chunks + 1 >= len(matmul_pieces)
    for k in range(n_chunks):
        if k < len(matmul_pieces):
            matmul_pieces[k]()
        rows = slice(k * c, (k + 1) * c)
        ya_ref[0, rows] = _hgrn_chunk(hg_scr[rows, 0:HG_WIDTH], hg_scr[rows, HG_WIDTH:2 * HG_WIDTH],
                                      hg_scr[rows, 2 * HG_WIDTH:3 * HG_WIDTH], hg_scr[rows, 3 * HG_WIDTH:n_hg],
                                      lb, gain, lvl, st_scr, g_scr).astype(ya_ref.dtype)
    for piece in matmul_pieces[n_chunks:]:
        piece()


def _in_projection(x, gain, sc, sh, w, lb_logits, hg_gain, layer):
    bsz, seq, d = x.shape
    tm = 512
    dils = [dil for _, dil in ATT_GROUPS]
    assert dils[0] == 1 and all(tm % (16 * dil) == 0 for dil in dils) and tm % HG_CHUNK == 0
    lvl = jnp.asarray(_hgrn_level_table(HG_CHUNK))

    def full(shape):
        return pl.BlockSpec(shape, lambda b, i: (0,) * len(shape), pipeline_mode=pl.Buffered(1))

    out_shape = [jax.ShapeDtypeStruct((bsz, seq, N_GATES), BF16)]
    out_specs = [pl.BlockSpec((1, tm, N_GATES), lambda b, i: (b, i, 0))]
    for dil in dils:
        out_shape.append(jax.ShapeDtypeStruct((bsz, dil, seq // dil, N_QKV), BF16))
        out_specs.append(pl.BlockSpec((1, dil, tm // dil, N_QKV), lambda b, i: (b, 0, i, 0)))
    out_shape.append(jax.ShapeDtypeStruct((bsz, seq, HG_WIDTH), BF16))
    out_specs.append(pl.BlockSpec((1, tm, HG_WIDTH), lambda b, i: (b, i, 0)))
    return pl.pallas_call(
        functools.partial(_inproj_kernel, layer=layer),
        grid=(bsz, seq // tm),
        in_specs=[pl.BlockSpec((1, tm, d), lambda b, i: (b, i, 0)),
                  full((1, d)),
                  pl.BlockSpec((1, 1, d), lambda b, i: (b, 0, 0)),
                  pl.BlockSpec((1, 1, d), lambda b, i: (b, 0, 0)),
                  full(w.shape), full(lb_logits.shape), full((1, HG_WIDTH)), full(lvl.shape)],
        out_specs=out_specs,
        out_shape=out_shape,
        scratch_shapes=[pltpu.VMEM((tm, 4 * HG_WIDTH), BF16),
                        pltpu.VMEM((d // LANES, tm, LANES), F32),
                        pltpu.VMEM((tm, d), BF16), pltpu.VMEM((tm, d), BF16),
                        pltpu.VMEM((HG_HEADS, HG_DIM, HG_DIM), F32),
                        pltpu.VMEM((HG_CHUNK, HG_WIDTH), F32)],
        compiler_params=_params("arbitrary", "arbitrary"),
        name="in_projection_hgrn2",
    )(x, gain.reshape(1, d), sc, sh, w, lb_logits, hg_gain.reshape(1, HG_WIDTH), lvl)


def _hgrn_level_table(c):
    t = np.arange(c)[:, None]
    s = np.arange(c)[None, :]
    x = t ^ s
    lvl = np.where(x > 0, np.floor(np.log2(np.maximum(x, 1))).astype(np.int32), 0)
    n_levels = int(np.log2(c))
    out = np.where(t > s, lvl, np.where(t == s, n_levels, -1)).astype(np.int32)
    return out


def _hgrn_lower_bound(lbl_ref, layer):
    a = lbl_ref[...]
    e = jnp.exp(a - jnp.max(a, axis=0, keepdims=True))
    return jnp.sum(e[:layer + 1], axis=0, keepdims=True) / jnp.sum(e, axis=0, keepdims=True)


def _hgrn_chunk(q_raw, f_raw, v, gate_raw, lb, gain, lvl, st_scr, g_scr):
    c = q_raw.shape[0]
    n_levels = c.bit_length() - 1

    q = _silu(q_raw.astype(F32))
    fg = lb + (1.0 - lb) * _sigmoid(f_raw.astype(F32))
    kk = 1.0 - fg
    logf = jnp.log(fg)

    r2 = lax.broadcasted_iota(jnp.int32, (c, c), 0)
    c2 = lax.broadcasted_iota(jnp.int32, (c, c), 1)
    tril = jnp.where(r2 >= c2, 1.0, 0.0).astype(BF16)
    hi = logf.astype(BF16)
    lo = (logf - hi.astype(F32)).astype(BF16)
    g = _dot(tril, hi) + _dot(tril, lo)
    g_scr[...] = g

    row = lax.broadcasted_iota(jnp.int32, (c, HG_DIM), 0)
    exps, rowbs = [], []
    for li in range(n_levels):
        h = 1 << li
        if h >= 4:
            parts = [jnp.broadcast_to(g_scr[pl.ds(p * 2 * h + h - 1, 1), :], (2 * h, HG_WIDTH))
                     for p in range(c // (2 * h))]
            e_l = -jnp.abs(g - jnp.concatenate(parts, axis=0))
            rowb = (row & (2 * h - 1)) >= h
        elif h == 2:
            up = pltpu.roll(logf, 1, 0)
            dn = pltpu.roll(logf, c - 1, 0)
            r4 = lax.broadcasted_iota(jnp.int32, (c, HG_WIDTH), 0) & 3
            e_l = jnp.where(r4 == 0, dn, jnp.where(r4 == 1, 0.0, jnp.where(r4 == 2, logf, up + logf)))
            rowb = (row & 3) >= 2
        else:
            r2w = lax.broadcasted_iota(jnp.int32, (c, HG_WIDTH), 0) & 1
            e_l = jnp.where(r2w == 1, logf, 0.0)
            rowb = (row & 1) == 1
        exps.append(jnp.exp(e_l))
        rowbs.append(rowb)

    eg = jnp.exp(g)
    g_last = g_scr[pl.ds(c - 1, 1), :]
    e_last = jnp.exp(g_last)
    kdec = kk * jnp.exp(g_last - g)
    gate = _silu(gate_raw.astype(F32))

    outs = []
    for hd in range(HG_HEADS):
        sl = slice(hd * HG_DIM, (hd + 1) * HG_DIM)
        qh, kh = q[:, sl], kk[:, sl]
        scores = jnp.where(lvl == n_levels, _dot_nt(qh.astype(BF16), kh.astype(BF16)), 0.0)
        for li in range(n_levels):
            xl = (jnp.where(rowbs[li], qh, kh) * exps[li][:, sl]).astype(BF16)
            scores = jnp.where(lvl == li, _dot_nt(xl, xl), scores)
        vh = v[:, sl]
        st = st_scr[hd]
        o = _dot(scores.astype(BF16), vh)
        o = o + _dot_nt((qh * eg[:, sl]).astype(BF16), st.astype(BF16))
        st_scr[hd] = st * e_last[:, sl] + _dot_tn(vh, kdec[:, sl].astype(BF16))
        ms = jnp.mean(o * o, axis=-1, keepdims=True)
        outs.append(o * lax.rsqrt(ms + EPS) * gain[:, sl] * gate[:, sl])
    return jnp.concatenate(outs, axis=1)


def _attn_kernel(q_ref, kp_ref, kc_ref, vp_ref, vc_ref, o_ref, lse_ref, kbuf, vbuf, o_scr, *, slopes, dil, band):
    blk = ATT_BLOCK
    n_mb = q_ref.shape[2] // blk
    step = pl.program_id(1)
    kbuf[:, 0:blk, :] = kp_ref[0]
    kbuf[:, blk:, :] = kc_ref[0]
    vbuf[:, 0:blk, :] = vp_ref[0]
    vbuf[:, blk:, :] = vc_ref[0]

    qi = lax.broadcasted_iota(jnp.int32, (blk, 2 * blk), 0) + blk
    ki = lax.broadcasted_iota(jnp.int32, (blk, 2 * blk), 1)
    dist = qi - ki
    in_band = (dist >= 0) & (dist <= band)
    distf = dist.astype(F32) * float(dil)

    def body(it, carry):
        mb = it // dil
        r = it - mb * dil
        row0 = pl.multiple_of(mb * blk, blk)
        first_key = jnp.where((step > 0) | (mb > 0), 0, blk)
        valid = in_band & (ki >= first_key)
        tok0 = mb * (blk * dil) + r
        rows = pl.ds(tok0, blk, stride=dil) if dil > 1 else pl.ds(tok0, blk)
        ones = jnp.ones((2 * blk, ATT_E), BF16)
        for j in range(ATT_HPG):
            sl = slice(j * ATT_E, (j + 1) * ATT_E)
            q = q_ref[0, r, pl.ds(row0, blk), sl]
            kw = kbuf[r, pl.ds(row0, 2 * blk), sl]
            vw = jnp.concatenate([vbuf[r, pl.ds(row0, 2 * blk), sl], ones], axis=1)
            s = _dot_nt(q, kw) * (ATT_E ** -0.5)
            s = jnp.where(valid, s - slopes[j] * distf, -jnp.inf)
            m = jnp.max(jnp.maximum(s[:, :blk], s[:, blk:]), axis=-1, keepdims=True)
            p = jnp.exp(s - m)
            ol = _dot(p.astype(BF16), vw)
            l = ol[:, ATT_E:]
            o_scr[j, rows, :] = ol[:, :ATT_E] / l
            lse_ref[0, j, rows, :] = m + jnp.log(l)
        return carry

    lax.fori_loop(0, n_mb * dil, body, 0, unroll=ATT_UNROLL)
    for j in range(ATT_HPG):
        o_ref[0, :, j * ATT_E:(j + 1) * ATT_E] = o_scr[j].astype(o_ref.dtype)


def _dilated_group(qkv, group):
    bsz, d, lsub, _ = qkv.shape
    seq = d * lsub
    window, dil = ATT_GROUPS[group]
    band = window // dil
    blk = ATT_BLOCK
    assert dil == d and band <= blk and ATT_SUPER % (blk * d) == 0 and seq % ATT_SUPER == 0
    rows = ATT_SUPER // d
    heads = np.arange(group * ATT_HPG, (group + 1) * ATT_HPG) + 1
    slopes = tuple(float(s) for s in np.exp2(-ALIBI_MAX * heads / N_ATT_HEADS))

    def cur(col):
        return pl.BlockSpec((1, d, rows, ATT_GW), lambda b, n: (b, 0, n, col))

    def prev(col):
        return pl.BlockSpec((1, d, blk, ATT_GW), lambda b, n: (b, 0, jnp.maximum(n * (rows // blk) - 1, 0), col))

    return pl.pallas_call(
        functools.partial(_attn_kernel, slopes=slopes, dil=d, band=band),
        grid=(bsz, seq // ATT_SUPER),
        in_specs=[cur(0), prev(1), cur(1), prev(2), cur(2)],
        out_specs=[pl.BlockSpec((1, ATT_SUPER, ATT_GW), lambda b, n: (b, n, 0)),
                   pl.BlockSpec((1, ATT_HPG, ATT_SUPER, ATT_E), lambda b, n: (b, 0, n, 0))],
        out_shape=[jax.ShapeDtypeStruct((bsz, seq, ATT_GW), BF16),
                   jax.ShapeDtypeStruct((bsz, ATT_HPG, seq, ATT_E), F32)],
        scratch_shapes=[pltpu.VMEM((d, rows + blk, ATT_GW), BF16), pltpu.VMEM((d, rows + blk, ATT_GW), BF16),
                        pltpu.VMEM((ATT_HPG, ATT_SUPER, ATT_E), F32)],
        compiler_params=_params("arbitrary", "arbitrary"),
        name=f"dilated_attn_g{group}",
    )(qkv, qkv, qkv, qkv, qkv)


def _mix_kernel(ya_ref, o1_ref, o2_ref, o3_ref, l1_ref, l2_ref, l3_ref, ga_ref, gb_ref, x_ref,
                wa_ref, wb_ref, wo_ref, gt_ref, sc_ref, sh_ref, gn_ref, x1_ref, h2_ref):
    parts = []
    for j in range(ATT_HPG):
        sl = slice(j * ATT_E, (j + 1) * ATT_E)
        l1, l2, l3 = l1_ref[0, j], l2_ref[0, j], l3_ref[0, j]
        m = jnp.maximum(jnp.maximum(l1, l2), l3)
        e1, e2, e3 = jnp.exp(l1 - m), jnp.exp(l2 - m), jnp.exp(l3 - m)
        num = (e1 * o1_ref[0, :, sl].astype(F32) + e2 * o2_ref[0, :, sl].astype(F32)
               + e3 * o3_ref[0, :, sl].astype(F32))
        parts.append(num / (e1 + e2 + e3))
    yb = jnp.concatenate(parts, axis=1)
    za = _dot(ya_ref[0], wa_ref[...])
    zb = _dot(yb.astype(BF16), wb_ref[...])
    merged = _gate_sigmoid(ga_ref[0].astype(F32)) * za + _gate_sigmoid(gb_ref[0].astype(F32)) * zb
    x1 = x_ref[0] + (1.0 + gt_ref[0]) * _dot(merged.astype(BF16), wo_ref[...])
    x1_ref[0] = x1
    ms = jnp.mean(x1 * x1, axis=-1, keepdims=True)
    scale = gn_ref[...] * (1.0 + sc_ref[0])
    h2_ref[0] = (x1 * lax.rsqrt(ms + EPS) * scale + sh_ref[0]).astype(h2_ref.dtype)


def _mix(ya, outs, lses, main, x, wa, wb, wo, gt, sc, sh, gn):
    bsz, seq, d = x.shape
    tm = 512

    def tok(width, colb=0):
        return pl.BlockSpec((1, tm, width), lambda b, i, colb=colb: (b, i, colb))

    def full(shape):
        return pl.BlockSpec(shape, lambda b, i: (0,) * len(shape), pipeline_mode=pl.Buffered(1))

    def per_batch():
        return pl.BlockSpec((1, 1, d), lambda b, i: (b, 0, 0))

    lse_spec = pl.BlockSpec((1, ATT_HPG, tm, ATT_E), lambda b, i: (b, 0, i, 0))
    return pl.pallas_call(
        _mix_kernel,
        grid=(bsz, seq // tm),
        in_specs=[tok(HG_WIDTH)] + [tok(ATT_GW)] * 3 + [lse_spec] * 3 + [tok(d, 0), tok(d, 1), tok(d),
                  full(wa.shape), full(wb.shape), full(wo.shape),
                  per_batch(), per_batch(), per_batch(), full((1, d))],
        out_specs=[tok(d), tok(d)],
        out_shape=[jax.ShapeDtypeStruct((bsz, seq, d), F32), jax.ShapeDtypeStruct((bsz, seq, d), BF16)],
        compiler_params=_params("arbitrary", "arbitrary"),
        name="mix",
    )(ya, *outs, *lses, main, main, x, wa, wb, wo, gt, sc, sh, gn.reshape(1, d))


def _ffn_kernel(h_ref, x_ref, wu_ref, wd_ref, cw_ref, cb_ref, gt_ref, gf_ref, o_ref, carry_scr, ug_scr, uv_scr,
                act_scr, *, final_norm):
    tm = h_ref.shape[1]
    fc = FFN_CHUNK

    @pl.when(pl.program_id(1) == 0)
    def _():
        carry_scr[...] = jnp.zeros_like(carry_scr)

    h = h_ref[0]
    row = lax.broadcasted_iota(jnp.int32, (tm, fc), 0)
    n_chunks = D_FF // fc

    def up_project(ci):
        ug_scr[ci % 2] = _dot(h, wu_ref[:, ci * fc:(ci + 1) * fc])
        uv_scr[ci % 2] = _dot(h, wu_ref[:, D_FF + ci * fc:D_FF + (ci + 1) * fc])

    up_project(0)
    for ci in range(n_chunks):
        if ci + 1 < n_chunks:
            up_project(ci + 1)
        cs = slice(ci * fc, (ci + 1) * fc)
        ug = ug_scr[ci % 2]
        prev2 = jnp.broadcast_to(carry_scr[6:7, cs], (tm, fc))
        prev1 = jnp.broadcast_to(carry_scr[7:8, cs], (tm, fc))
        r1 = pltpu.roll(ug, 1, 0)
        r2 = pltpu.roll(ug, 2, 0)
        u1 = jnp.where(row == 0, prev1, r1)
        u2 = jnp.where(row == 0, prev2, jnp.where(row == 1, prev1, r2))
        carry_scr[:, cs] = ug[tm - 8:tm, :]
        y = cb_ref[:, cs] + u2 * cw_ref[0:1, cs] + u1 * cw_ref[1:2, cs] + ug * cw_ref[2:3, cs]
        gel = 0.5 * y * (1.0 + jnp.tanh(0.7978845608028654 * (y + 0.044715 * (y * y * y))))
        act_scr[:, cs] = (gel * uv_scr[ci % 2]).astype(BF16)
    x2 = x_ref[0] + (1.0 + gt_ref[0]) * _dot(act_scr[...], wd_ref[...])
    if final_norm:
        ms = jnp.mean(x2 * x2, axis=-1, keepdims=True)
        x2 = x2 * lax.rsqrt(ms + EPS) * gf_ref[...]
    o_ref[0] = x2


def _ffn(h2, x1, wu, wd, cw, cb, gt, gf, final_norm):
    bsz, seq, d = x1.shape
    tm = 512

    def tok():
        return pl.BlockSpec((1, tm, d), lambda b, i: (b, i, 0))

    def full(shape):
        return pl.BlockSpec(shape, lambda b, i: (0,) * len(shape), pipeline_mode=pl.Buffered(1))

    return pl.pallas_call(
        functools.partial(_ffn_kernel, final_norm=final_norm),
        grid=(bsz, seq // tm),
        in_specs=[tok(), tok(), full(wu.shape), full(wd.shape), full(cw.shape), full((1, D_FF)),
                  pl.BlockSpec((1, 1, d), lambda b, i: (b, 0, 0)), full((1, d))],
        out_specs=tok(),
        out_shape=jax.ShapeDtypeStruct((bsz, seq, d), F32),
        scratch_shapes=[pltpu.VMEM((8, D_FF), F32), pltpu.VMEM((2, tm, FFN_CHUNK), F32),
                        pltpu.VMEM((2, tm, FFN_CHUNK), F32), pltpu.VMEM((tm, D_FF), BF16)],
        compiler_params=_params("arbitrary", "arbitrary"),
        name="ffn",
    )(h2, x1, wu, wd, cw, cb.reshape(1, D_FF), gt, gf.reshape(1, d))


def kernel(x, c, w_ada, b_ada, g_norm_mix, w_in, lb_logits, g_hg_norm, w_branch_a, w_branch_b, w_out, g_norm_ffn,
           w_up, conv_w, conv_b, w_down, g_final):
    depth = w_ada.shape[0]
    bsz = x.shape[0]
    n_hg = 4 * HG_WIDTH
    for l in range(depth):
        mod = _modulation(c, w_ada[l], b_ada[l]).reshape(bsz, N_MOD, 1, D_MODEL)
        sh1, sc1, gt1, sh2, sc2, gt2 = [mod[:, k] for k in range(N_MOD)]
        w = w_in[l]
        n_grp = len(ATT_GROUPS)
        o_q = n_hg
        o_g = n_hg + 3 * n_grp * ATT_GW
        cols = [w[:, o_g:], w[:, :n_hg]]
        for g in range(n_grp):
            cols += [w[:, o_q + (t * n_grp + g) * ATT_GW: o_q + (t * n_grp + g + 1) * ATT_GW] for t in range(3)]
        w_perm = jnp.concatenate(cols, axis=1).astype(BF16)

        gates, *qkvs, ya = _in_projection(x, g_norm_mix[l], sc1, sh1, w_perm, lb_logits,
                                          g_hg_norm[l].reshape(HG_WIDTH), l)
        groups = [_dilated_group(qkvs[g], g) for g in range(n_grp)]
        x1, h2 = _mix(ya, [o for o, _ in groups], [s for _, s in groups], gates, x,
                      w_branch_a[l].astype(BF16), w_branch_b[l].astype(BF16), w_out[l].astype(BF16),
                      gt1, sc2, sh2, g_norm_ffn[l])
        x = _ffn(h2, x1, w_up[l].astype(BF16), w_down[l].astype(BF16), conv_w[l], conv_b[l], gt2, g_final,
                 final_norm=(l == depth - 1))
    return x
```

```python
import functools

import numpy as np
import jax
import jax.numpy as jnp
from jax import lax
from jax.experimental import pallas as pl
from jax.experimental.pallas import tpu as pltpu

F32 = jnp.float32
BF16 = jnp.bfloat16

D_MODEL = 1024
HG_HEADS = 4
HG_DIM = 128
HG_WIDTH = HG_HEADS * HG_DIM
ATT_GROUPS = ((128, 1), (512, 4), (2048, 16))
ATT_HPG = 4
ATT_E = 128
N_ATT_HEADS = ATT_HPG * len(ATT_GROUPS)
ATT_GW = ATT_HPG * ATT_E
ATT_BLOCK = 128
ALIBI_MAX = 8.0
D_FF = 2816
N_MOD = 6
EPS = 1e-6
LOG2_E = 1.4426950408889634

N_GATES = 2 * D_MODEL
N_QKV = 3 * ATT_GW
ATT_SUPER = 2048
ATT_UNROLL = 8
LANES = 128

VMEM_LIMIT_BYTES = 56 * 1024 * 1024

HG_CHUNK = 128
FFN_CHUNK = 256


def _sigmoid(x):
    return 1.0 / (1.0 + jnp.exp(-x))


def _gate_sigmoid(x):
    return 0.5 * jnp.tanh(0.5 * x) + 0.5


def _silu(x):
    return x * _gate_sigmoid(x)


def _dot(a, b):
    return jnp.dot(a, b, preferred_element_type=F32)


def _dot_nt(a, b):
    return lax.dot_general(a, b, (((1,), (1,)), ((), ())), preferred_element_type=F32)


def _dot_tn(a, b):
    return lax.dot_general(a, b, (((0,), (0,)), ((), ())), preferred_element_type=F32)


def _params(*sem):
    return pltpu.CompilerParams(dimension_semantics=sem, vmem_limit_bytes=VMEM_LIMIT_BYTES)


def _mod_kernel(c_ref, w_ref, b_ref, o_ref):
    s = _silu(c_ref[...]).astype(BF16)
    o_ref[...] = _dot(s, w_ref[...].astype(BF16)) + b_ref[...]


def _modulation(c, w, b):
    bsz, d = c.shape
    n = w.shape[1]
    tn = 1024
    return pl.pallas_call(
        _mod_kernel,
        grid=(n // tn,),
        in_specs=[pl.BlockSpec((bsz, d), lambda j: (0, 0)),
                  pl.BlockSpec((d, tn), lambda j: (0, j)),
                  pl.BlockSpec((1, tn), lambda j: (0, j))],
        out_specs=pl.BlockSpec((bsz, tn), lambda j: (0, j)),
        out_shape=jax.ShapeDtypeStruct((bsz, n), F32),
        compiler_params=_params("arbitrary"),
        name="modulation",
    )(c, w, b.reshape(1, n))


def _inproj_kernel(x_ref, g_ref, sc_ref, sh_ref, w_ref, lbl_ref, hgain_ref, lvl_ref,
                   gates_ref, a0_ref, a1_ref, a2_ref, ya_ref,
                   hg_scr, slab_scr, hp1_scr, hp2_scr, st_scr, g_scr, *, layer):
    tm, d_model = x_ref.shape[1], x_ref.shape[2]

    @pl.when(pl.program_id(1) == 0)
    def _():
        st_scr[...] = jnp.zeros_like(st_scr)

    x = x_ref[0]
    ms = jnp.mean(x * x, axis=-1, keepdims=True)
    scale = g_ref[...] * (1.0 + sc_ref[0])
    h = x * lax.rsqrt(ms + EPS) * scale + sh_ref[0]
    hb = h.astype(BF16)
    n_slabs = d_model // LANES
    for s in range(n_slabs):
        slab_scr[s] = h[:, s * LANES:(s + 1) * LANES]

    n_hg = 4 * HG_WIDTH
    for c0 in range(0, n_hg, 1024):
        hg_scr[:, c0:c0 + 1024] = _dot(hb, w_ref[:, N_GATES + c0:N_GATES + c0 + 1024]).astype(BF16)

    def gate_columns(c0):
        gates_ref[0, :, c0:c0 + 1024] = _dot(hb, w_ref[:, c0:c0 + 1024]).astype(gates_ref.dtype)

    def group0():
        c0 = N_GATES + n_hg
        a0_ref[0, 0] = _dot(hb, w_ref[:, c0:c0 + N_QKV]).astype(a0_ref.dtype)

    def dilated_group(gi, out_ref, hp_scr):
        dil = ATT_GROUPS[gi][1]
        rows = tm // dil
        for r in range(dil):
            piece = jnp.concatenate([slab_scr[s, pl.ds(r, rows, stride=dil), :] for s in range(n_slabs)], axis=1)
            hp_scr[r * rows:(r + 1) * rows, :] = piece.astype(BF16)
        c0 = N_GATES + n_hg + gi * N_QKV
        res = _dot(hp_scr[...], w_ref[:, c0:c0 + N_QKV]).astype(out_ref.dtype)
        for r in range(dil):
            out_ref[0, r] = res[r * rows:(r + 1) * rows, :]

    matmul_pieces = [lambda: gate_columns(0), lambda: gate_columns(1024), group0,
                     lambda: dilated_group(1, a1_ref, hp1_scr), lambda: dilated_group(2, a2_ref, hp2_scr)]
    lb = _hgrn_lower_bound(lbl_ref, layer)
    gain = hgain_ref[...]
    lvl = lvl_ref[...]
    c = HG_CHUNK
    n_chunks = tm // c
    assert n_chunks + 1 >= len(matmul_pieces)
    for k in range(n_chunks):
        if k < len(matmul_pieces):
            matmul_pieces[k]()
        rows = slice(k * c, (k + 1) * c)
        ya_ref[0, rows] = _hgrn_chunk(hg_scr[rows, 0:HG_WIDTH], hg_scr[rows, HG_WIDTH:2 * HG_WIDTH],
                                      hg_scr[rows, 2 * HG_WIDTH:3 * HG_WIDTH], hg_scr[rows, 3 * HG_WIDTH:n_hg],
                                      lb, gain, lvl, st_scr, g_scr).astype(ya_ref.dtype)
    for piece in matmul_pieces[n_chunks:]:
        piece()


def _in_projection(x, gain, sc, sh, w, lb_logits, hg_gain, layer):
    bsz, seq, d = x.shape
    tm = 512
    dils = [dil for _, dil in ATT_GROUPS]
    assert dils[0] == 1 and all(tm % (16 * dil) == 0 for dil in dils) and tm % HG_CHUNK == 0
    lvl = jnp.asarray(_hgrn_level_table(HG_CHUNK))

    def full(shape):
        return pl.BlockSpec(shape, lambda b, i: (0,) * len(shape), pipeline_mode=pl.Buffered(1))

    out_shape = [jax.ShapeDtypeStruct((bsz, seq, N_GATES), BF16)]
    out_specs = [pl.BlockSpec((1, tm, N_GATES), lambda b, i: (b, i, 0))]
    for dil in dils:
        out_shape.append(jax.ShapeDtypeStruct((bsz, dil, seq // dil, N_QKV), BF16))
        out_specs.append(pl.BlockSpec((1, dil, tm // dil, N_QKV), lambda b, i: (b, 0, i, 0)))
    out_shape.append(jax.ShapeDtypeStruct((bsz, seq, HG_WIDTH), BF16))
    out_specs.append(pl.BlockSpec((1, tm, HG_WIDTH), lambda b, i: (b, i, 0)))
    return pl.pallas_call(
        functools.partial(_inproj_kernel, layer=layer),
        grid=(bsz, seq // tm),
        in_specs=[pl.BlockSpec((1, tm, d), lambda b, i: (b, i, 0)),
                  full((1, d)),
                  pl.BlockSpec((1, 1, d), lambda b, i: (b, 0, 0)),
                  pl.BlockSpec((1, 1, d), lambda b, i: (b, 0, 0)),
                  full(w.shape), full(lb_logits.shape), full((1, HG_WIDTH)), full(lvl.shape)],
        out_specs=out_specs,
        out_shape=out_shape,
        scratch_shapes=[pltpu.VMEM((tm, 4 * HG_WIDTH), BF16),
                        pltpu.VMEM((d // LANES, tm, LANES), F32),
                        pltpu.VMEM((tm, d), BF16), pltpu.VMEM((tm, d), BF16),
                        pltpu.VMEM((HG_HEADS, HG_DIM, HG_DIM), F32),
                        pltpu.VMEM((HG_CHUNK, HG_WIDTH), F32)],
        compiler_params=_params("arbitrary", "arbitrary"),
        name="in_projection_hgrn2",
    )(x, gain.reshape(1, d), sc, sh, w, lb_logits, hg_gain.reshape(1, HG_WIDTH), lvl)


def _hgrn_level_table(c):
    t = np.arange(c)[:, None]
    s = np.arange(c)[None, :]
    x = t ^ s
    lvl = np.where(x > 0, np.floor(np.log2(np.maximum(x, 1))).astype(np.int32), 0)
    n_levels = int(np.log2(c))
    out = np.where(t > s, lvl, np.where(t == s, n_levels, -1)).astype(np.int32)
    return out


def _hgrn_lower_bound(lbl_ref, layer):
    a = lbl_ref[...]
    e = jnp.exp(a - jnp.max(a, axis=0, keepdims=True))
    return jnp.sum(e[:layer + 1], axis=0, keepdims=True) / jnp.sum(e, axis=0, keepdims=True)


def _hgrn_chunk(q_raw, f_raw, v, gate_raw, lb, gain, lvl, st_scr, g_scr):
    c = q_raw.shape[0]
    n_levels = c.bit_length() - 1

    q = _silu(q_raw.astype(F32))
    fg = lb + (1.0 - lb) * _sigmoid(f_raw.astype(F32))
    kk = 1.0 - fg
    logf = jnp.log(fg) * LOG2_E

    r2 = lax.broadcasted_iota(jnp.int32, (c, c), 0)
    c2 = lax.broadcasted_iota(jnp.int32, (c, c), 1)
    tril = jnp.where(r2 >= c2, 1.0, 0.0).astype(BF16)
    hi = logf.astype(BF16)
    lo = (logf - hi.astype(F32)).astype(BF16)
    g = _dot(tril, hi) + _dot(tril, lo)
    g_scr[...] = g

    row = lax.broadcasted_iota(jnp.int32, (c, HG_DIM), 0)
    lane_tiles = [slice(hd * HG_DIM, (hd + 1) * HG_DIM) for hd in range(HG_HEADS)]
    exps, rowbs = [], []
    for li in range(n_levels):
        h = 1 << li
        if h >= 4:
            parts = [jnp.broadcast_to(g_scr[pl.ds(p * 2 * h + h - 1, 1), :], (2 * h, HG_WIDTH))
                     for p in range(c // (2 * h))]
            e_l = -jnp.abs(g - jnp.concatenate(parts, axis=0))
            rowb = None if h >= 8 else (row & (2 * h - 1)) >= h
        elif h == 2:
            up = pltpu.roll(logf, 1, 0)
            dn = pltpu.roll(logf, c - 1, 0)
            r4 = row & 3
            m0, m1, m2 = r4 == 0, r4 == 1, r4 == 2
            e_l = jnp.concatenate(
                [jnp.where(m0, dn[:, sl], jnp.where(m1, 0.0, jnp.where(m2, logf[:, sl], up[:, sl] + logf[:, sl])))
                 for sl in lane_tiles], axis=1)
            rowb = r4 >= 2
        else:
            rowb = (row & 1) == 1
            e_l = jnp.concatenate([jnp.where(rowb, logf[:, sl], 0.0) for sl in lane_tiles], axis=1)
        exps.append(jnp.exp2(e_l))
        rowbs.append(rowb)

    eg = jnp.exp2(g)
    g_last = g_scr[pl.ds(c - 1, 1), :]
    e_last = jnp.exp2(g_last)
    kdec = kk * jnp.exp2(g_last - g)
    gate = _silu(gate_raw.astype(F32))

    def query_or_key_rows(li, qh, kh):
        h = 1 << li
        if rowbs[li] is not None:
            return jnp.where(rowbs[li], qh, kh)
        blocks = []
        for p in range(c // (2 * h)):
            blocks += [kh[2 * p * h:(2 * p + 1) * h], qh[(2 * p + 1) * h:(2 * p + 2) * h]]
        return jnp.concatenate(blocks, axis=0)

    outs = []
    for hd in range(HG_HEADS):
        sl = lane_tiles[hd]
        qh, kh = q[:, sl], kk[:, sl]
        scores = jnp.where(lvl == n_levels, _dot_nt(qh.astype(BF16), kh.astype(BF16)), 0.0)
        for li in range(n_levels):
            xl = (query_or_key_rows(li, qh, kh) * exps[li][:, sl]).astype(BF16)
            scores = jnp.where(lvl == li, _dot_nt(xl, xl), scores)
        vh = v[:, sl]
        st = st_scr[hd]
        o = _dot(scores.astype(BF16), vh)
        o = o + _dot_nt((qh * eg[:, sl]).astype(BF16), st.astype(BF16))
        st_scr[hd] = st * e_last[:, sl] + _dot_tn(vh, kdec[:, sl].astype(BF16))
        ms = jnp.mean(o * o, axis=-1, keepdims=True)
        outs.append(o * lax.rsqrt(ms + EPS) * gain[:, sl] * gate[:, sl])
    return jnp.concatenate(outs, axis=1)


def _attn_bias_tables(group):
    window, dil = ATT_GROUPS[group]
    band = window // dil
    blk = ATT_BLOCK
    heads = np.arange(group * ATT_HPG, (group + 1) * ATT_HPG) + 1
    slopes = np.exp2(-ALIBI_MAX * heads / N_ATT_HEADS).astype(np.float32)
    qi = np.arange(blk)[:, None] + blk
    ki = np.arange(2 * blk)[None, :]
    dist = qi - ki
    in_band = (dist >= 0) & (dist <= band)
    bias = -(slopes[:, None, None] * (dist.astype(np.float32) * dil)[None]) * np.float32(LOG2_E)
    tables = [np.where(in_band[None], bias, -np.inf), np.where((in_band & (ki >= blk))[None], bias, -np.inf)]
    return np.stack(tables).astype(np.float32)


def _attn_kernel(q_ref, kp_ref, kc_ref, vp_ref, vc_ref, bias_ref, o_ref, lse_ref, kbuf, vbuf, o_scr, *, dil):
    blk = ATT_BLOCK
    n_mb = q_ref.shape[2] // blk
    step = pl.program_id(1)
    kbuf[:, 0:blk, :] = kp_ref[0]
    kbuf[:, blk:, :] = kc_ref[0]
    vbuf[:, 0:blk, :] = vp_ref[0]
    vbuf[:, blk:, :] = vc_ref[0]

    def body(it, carry):
        mb = it // dil
        r = it - mb * dil
        row0 = pl.multiple_of(mb * blk, blk)
        table = jnp.where((step > 0) | (mb > 0), 0, 1)
        tok0 = mb * (blk * dil) + r
        rows = pl.ds(tok0, blk, stride=dil) if dil > 1 else pl.ds(tok0, blk)
        ones = jnp.ones((2 * blk, ATT_E), BF16)
        for j in range(ATT_HPG):
            sl = slice(j * ATT_E, (j + 1) * ATT_E)
            q = q_ref[0, r, pl.ds(row0, blk), sl]
            kw = kbuf[r, pl.ds(row0, 2 * blk), sl]
            vw = jnp.concatenate([vbuf[r, pl.ds(row0, 2 * blk), sl], ones], axis=1)
            s = _dot_nt(q, kw) * (ATT_E ** -0.5 * LOG2_E) + bias_ref[table, j]
            m = jnp.max(jnp.maximum(s[:, :blk], s[:, blk:]), axis=-1, keepdims=True)
            p = jnp.exp2(s - m)
            ol = _dot(p.astype(BF16), vw)
            l = ol[:, ATT_E:]
            o_scr[j, rows, :] = ol[:, :ATT_E] / l
            lse_ref[0, j, rows, :] = m + jnp.log(l) * LOG2_E
        return carry

    lax.fori_loop(0, n_mb * dil, body, 0, unroll=ATT_UNROLL)
    for j in range(ATT_HPG):
        o_ref[0, :, j * ATT_E:(j + 1) * ATT_E] = o_scr[j].astype(o_ref.dtype)


def _dilated_group(qkv, group):
    bsz, d, lsub, _ = qkv.shape
    seq = d * lsub
    window, dil = ATT_GROUPS[group]
    band = window // dil
    blk = ATT_BLOCK
    assert dil == d and band <= blk and ATT_SUPER % (blk * d) == 0 and seq % ATT_SUPER == 0
    rows = ATT_SUPER // d
    bias = jnp.asarray(_attn_bias_tables(group))

    def cur(col):
        return pl.BlockSpec((1, d, rows, ATT_GW), lambda b, n: (b, 0, n, col))

    def prev(col):
        return pl.BlockSpec((1, d, blk, ATT_GW), lambda b, n: (b, 0, jnp.maximum(n * (rows // blk) - 1, 0), col))

    return pl.pallas_call(
        functools.partial(_attn_kernel, dil=d),
        grid=(bsz, seq // ATT_SUPER),
        in_specs=[cur(0), prev(1), cur(1), prev(2), cur(2),
                  pl.BlockSpec(bias.shape, lambda b, n: (0, 0, 0, 0), pipeline_mode=pl.Buffered(1))],
        out_specs=[pl.BlockSpec((1, ATT_SUPER, ATT_GW), lambda b, n: (b, n, 0)),
                   pl.BlockSpec((1, ATT_HPG, ATT_SUPER, ATT_E), lambda b, n: (b, 0, n, 0))],
        out_shape=[jax.ShapeDtypeStruct((bsz, seq, ATT_GW), BF16),
                   jax.ShapeDtypeStruct((bsz, ATT_HPG, seq, ATT_E), F32)],
        scratch_shapes=[pltpu.VMEM((d, rows + blk, ATT_GW), BF16), pltpu.VMEM((d, rows + blk, ATT_GW), BF16),
                        pltpu.VMEM((ATT_HPG, ATT_SUPER, ATT_E), F32)],
        compiler_params=_params("arbitrary", "arbitrary"),
        name=f"dilated_attn_g{group}",
    )(qkv, qkv, qkv, qkv, qkv, bias)


def _mix_kernel(ya_ref, o1_ref, o2_ref, o3_ref, l1_ref, l2_ref, l3_ref, ga_ref, gb_ref, x_ref,
                wa_ref, wb_ref, wo_ref, gt_ref, sc_ref, sh_ref, gn_ref, x1_ref, h2_ref):
    parts = []
    for j in range(ATT_HPG):
        sl = slice(j * ATT_E, (j + 1) * ATT_E)
        l1, l2, l3 = l1_ref[0, j], l2_ref[0, j], l3_ref[0, j]
        m = jnp.maximum(jnp.maximum(l1, l2), l3)
        e1, e2, e3 = jnp.exp2(l1 - m), jnp.exp2(l2 - m), jnp.exp2(l3 - m)
        num = (e1 * o1_ref[0, :, sl].astype(F32) + e2 * o2_ref[0, :, sl].astype(F32)
               + e3 * o3_ref[0, :, sl].astype(F32))
        parts.append(num / (e1 + e2 + e3))
    yb = jnp.concatenate(parts, axis=1)
    za = _dot(ya_ref[0], wa_ref[...])
    zb = _dot(yb.astype(BF16), wb_ref[...])
    ta = jnp.tanh((0.5 * ga_ref[0]).astype(F32))
    tb = jnp.tanh((0.5 * gb_ref[0]).astype(F32))
    merged2 = (ta + 1.0) * za + (tb + 1.0) * zb
    x1 = x_ref[0] + (0.5 * (1.0 + gt_ref[0])) * _dot(merged2.astype(BF16), wo_ref[...])
    x1_ref[0] = x1
    ms = jnp.mean(x1 * x1, axis=-1, keepdims=True)
    scale = gn_ref[...] * (1.0 + sc_ref[0])
    h2_ref[0] = (x1 * lax.rsqrt(ms + EPS) * scale + sh_ref[0]).astype(h2_ref.dtype)


def _mix(ya, outs, lses, main, x, wa, wb, wo, gt, sc, sh, gn):
    bsz, seq, d = x.shape
    tm = 512

    def tok(width, colb=0):
        return pl.BlockSpec((1, tm, width), lambda b, i, colb=colb: (b, i, colb))

    def full(shape):
        return pl.BlockSpec(shape, lambda b, i: (0,) * len(shape), pipeline_mode=pl.Buffered(1))

    def per_batch():
        return pl.BlockSpec((1, 1, d), lambda b, i: (b, 0, 0))

    lse_spec = pl.BlockSpec((1, ATT_HPG, tm, ATT_E), lambda b, i: (b, 0, i, 0))
    return pl.pallas_call(
        _mix_kernel,
        grid=(bsz, seq // tm),
        in_specs=[tok(HG_WIDTH)] + [tok(ATT_GW)] * 3 + [lse_spec] * 3 + [tok(d, 0), tok(d, 1), tok(d),
                  full(wa.shape), full(wb.shape), full(wo.shape),
                  per_batch(), per_batch(), per_batch(), full((1, d))],
        out_specs=[tok(d), tok(d)],
        out_shape=[jax.ShapeDtypeStruct((bsz, seq, d), F32), jax.ShapeDtypeStruct((bsz, seq, d), BF16)],
        compiler_params=_params("arbitrary", "arbitrary"),
        name="mix",
    )(ya, *outs, *lses, main, main, x, wa, wb, wo, gt, sc, sh, gn.reshape(1, d))


def _ffn_kernel(h_ref, x_ref, wu_ref, wd_ref, cw_ref, cb_ref, gt_ref, gf_ref, o_ref, carry_scr, ug_scr, uv_scr,
                act_scr, *, final_norm):
    tm = h_ref.shape[1]
    fc = FFN_CHUNK

    @pl.when(pl.program_id(1) == 0)
    def _():
        carry_scr[...] = jnp.zeros_like(carry_scr)

    h = h_ref[0]
    row = lax.broadcasted_iota(jnp.int32, (tm, fc), 0)
    n_chunks = D_FF // fc

    def up_project(ci):
        ug_scr[ci % 2] = _dot(h, wu_ref[:, ci * fc:(ci + 1) * fc])
        uv_scr[ci % 2] = _dot(h, wu_ref[:, D_FF + ci * fc:D_FF + (ci + 1) * fc])

    up_project(0)
    for ci in range(n_chunks):
        if ci + 1 < n_chunks:
            up_project(ci + 1)
        cs = slice(ci * fc, (ci + 1) * fc)
        ug = ug_scr[ci % 2]
        prev2 = jnp.broadcast_to(carry_scr[6:7, cs], (tm, fc))
        prev1 = jnp.broadcast_to(carry_scr[7:8, cs], (tm, fc))
        r1 = pltpu.roll(ug, 1, 0)
        r2 = pltpu.roll(ug, 2, 0)
        u1 = jnp.where(row == 0, prev1, r1)
        u2 = jnp.where(row == 0, prev2, jnp.where(row == 1, prev1, r2))
        carry_scr[:, cs] = ug[tm - 8:tm, :]
        y = cb_ref[:, cs] + u2 * cw_ref[0:1, cs] + u1 * cw_ref[1:2, cs] + ug * cw_ref[2:3, cs]
        gel = 0.5 * y * (1.0 + jnp.tanh(0.7978845608028654 * (y + 0.044715 * (y * y * y))))
        act_scr[:, cs] = (gel * uv_scr[ci % 2]).astype(BF16)
    x2 = x_ref[0] + (1.0 + gt_ref[0]) * _dot(act_scr[...], wd_ref[...])
    if final_norm:
        ms = jnp.mean(x2 * x2, axis=-1, keepdims=True)
        x2 = x2 * lax.rsqrt(ms + EPS) * gf_ref[...]
    o_ref[0] = x2


def _ffn(h2, x1, wu, wd, cw, cb, gt, gf, final_norm):
    bsz, seq, d = x1.shape
    tm = 1024

    def tok():
        return pl.BlockSpec((1, tm, d), lambda b, i: (b, i, 0))

    def full(shape):
        return pl.BlockSpec(shape, lambda b, i: (0,) * len(shape), pipeline_mode=pl.Buffered(1))

    return pl.pallas_call(
        functools.partial(_ffn_kernel, final_norm=final_norm),
        grid=(bsz, seq // tm),
        in_specs=[tok(), tok(), full(wu.shape), full(wd.shape), full(cw.shape), full((1, D_FF)),
                  pl.BlockSpec((1, 1, d), lambda b, i: (b, 0, 0)), full((1, d))],
        out_specs=tok(),
        out_shape=jax.ShapeDtypeStruct((bsz, seq, d), F32),
        scratch_shapes=[pltpu.VMEM((8, D_FF), F32), pltpu.VMEM((2, tm, FFN_CHUNK), F32),
                        pltpu.VMEM((2, tm, FFN_CHUNK), F32), pltpu.VMEM((tm, D_FF), BF16)],
        compiler_params=_params("arbitrary", "arbitrary"),
        name="ffn",
    )(h2, x1, wu, wd, cw, cb.reshape(1, D_FF), gt, gf.reshape(1, d))


def kernel(x, c, w_ada, b_ada, g_norm_mix, w_in, lb_logits, g_hg_norm, w_branch_a, w_branch_b, w_out, g_norm_ffn,
           w_up, conv_w, conv_b, w_down, g_final):
    depth = w_ada.shape[0]
    bsz = x.shape[0]
    n_hg = 4 * HG_WIDTH
    for l in range(depth):
        mod = _modulation(c, w_ada[l], b_ada[l]).reshape(bsz, N_MOD, 1, D_MODEL)
        sh1, sc1, gt1, sh2, sc2, gt2 = [mod[:, k] for k in range(N_MOD)]
        w = w_in[l]
        n_grp = len(ATT_GROUPS)
        o_q = n_hg
        o_g = n_hg + 3 * n_grp * ATT_GW
        cols = [w[:, o_g:], w[:, :n_hg]]
        for g in range(n_grp):
            cols += [w[:, o_q + (t * n_grp + g) * ATT_GW: o_q + (t * n_grp + g + 1) * ATT_GW] for t in range(3)]
        w_perm = jnp.concatenate(cols, axis=1).astype(BF16)

        gates, *qkvs, ya = _in_projection(x, g_norm_mix[l], sc1, sh1, w_perm, lb_logits,
                                          g_hg_norm[l].reshape(HG_WIDTH), l)
        groups = [_dilated_group(qkvs[g], g) for g in range(n_grp)]
        x1, h2 = _mix(ya, [o for o, _ in groups], [s for _, s in groups], gates, x,
                      w_branch_a[l].astype(BF16), w_branch_b[l].astype(BF16), w_out[l].astype(BF16),
                      gt1, sc2, sh2, g_norm_ffn[l])
        x = _ffn(h2, x1, w_up[l].astype(BF16), w_down[l].astype(BF16), conv_w[l], conv_b[l], gt2, g_final,
                 final_norm=(l == depth - 1))
    return x
```

```python
import functools

import numpy as np
import jax
import jax.numpy as jnp
from jax import lax
from jax.experimental import pallas as pl
from jax.experimental.pallas import tpu as pltpu

F32 = jnp.float32
BF16 = jnp.bfloat16

D_MODEL = 1024
HG_HEADS = 4
HG_DIM = 128
HG_WIDTH = HG_HEADS * HG_DIM
ATT_GROUPS = ((128, 1), (512, 4), (2048, 16))
ATT_HPG = 4
ATT_E = 128
N_ATT_HEADS = ATT_HPG * len(ATT_GROUPS)
ATT_GW = ATT_HPG * ATT_E
ATT_BLOCK = 128
ALIBI_MAX = 8.0
D_FF = 2816
N_MOD = 6
EPS = 1e-6
LOG2_E = 1.4426950408889634

N_GATES = 2 * D_MODEL
N_QKV = 3 * ATT_GW
ATT_SUPER = 2048
ATT_UNROLL = 8
LANES = 128

VMEM_LIMIT_BYTES = 56 * 1024 * 1024

HG_CHUNK = 256
FFN_CHUNK = 256


def _sigmoid(x):
    return 1.0 / (1.0 + jnp.exp(-x))


def _gate_sigmoid(x):
    return 0.5 * jnp.tanh(0.5 * x) + 0.5


def _silu(x):
    return x * _gate_sigmoid(x)


def _dot(a, b):
    return jnp.dot(a, b, preferred_element_type=F32)


def _dot_nt(a, b):
    return lax.dot_general(a, b, (((1,), (1,)), ((), ())), preferred_element_type=F32)


def _dot_tn(a, b):
    return lax.dot_general(a, b, (((0,), (0,)), ((), ())), preferred_element_type=F32)


def _params(*sem):
    return pltpu.CompilerParams(dimension_semantics=sem, vmem_limit_bytes=VMEM_LIMIT_BYTES)


def _mod_kernel(c_ref, w_ref, b_ref, o_ref):
    s = _silu(c_ref[...]).astype(BF16)
    o_ref[...] = _dot(s, w_ref[...].astype(BF16)) + b_ref[...]


def _modulation(c, w, b):
    bsz, d = c.shape
    n = w.shape[1]
    tn = 1024
    return pl.pallas_call(
        _mod_kernel,
        grid=(n // tn,),
        in_specs=[pl.BlockSpec((bsz, d), lambda j: (0, 0)),
                  pl.BlockSpec((d, tn), lambda j: (0, j)),
                  pl.BlockSpec((1, tn), lambda j: (0, j))],
        out_specs=pl.BlockSpec((bsz, tn), lambda j: (0, j)),
        out_shape=jax.ShapeDtypeStruct((bsz, n), F32),
        compiler_params=_params("arbitrary"),
        name="modulation",
    )(c, w, b.reshape(1, n))


def _inproj_kernel(x_ref, g_ref, sc_ref, sh_ref, w_ref, lbl_ref, hgain_ref, lvl_ref,
                   gates_ref, a0_ref, a1_ref, a2_ref, ya_ref,
                   hg_scr, slab_scr, hp1_scr, hp2_scr, st_scr, g_scr, *, layer):
    tm, d_model = x_ref.shape[1], x_ref.shape[2]

    @pl.when(pl.program_id(1) == 0)
    def _():
        st_scr[...] = jnp.zeros_like(st_scr)

    x = x_ref[0]
    ms = jnp.mean(x * x, axis=-1, keepdims=True)
    scale = g_ref[...] * (1.0 + sc_ref[0])
    h = x * lax.rsqrt(ms + EPS) * scale + sh_ref[0]
    hb = h.astype(BF16)
    n_slabs = d_model // LANES
    for s in range(n_slabs):
        slab_scr[s] = h[:, s * LANES:(s + 1) * LANES]

    n_hg = 4 * HG_WIDTH
    for c0 in range(0, n_hg, 1024):
        hg_scr[:, c0:c0 + 1024] = _dot(hb, w_ref[:, N_GATES + c0:N_GATES + c0 + 1024]).astype(BF16)

    def gate_columns(c0):
        gates_ref[0, :, c0:c0 + 1024] = _dot(hb, w_ref[:, c0:c0 + 1024]).astype(gates_ref.dtype)

    def group0():
        c0 = N_GATES + n_hg
        a0_ref[0, 0] = _dot(hb, w_ref[:, c0:c0 + N_QKV]).astype(a0_ref.dtype)

    def dilated_group(gi, out_ref, hp_scr):
        dil = ATT_GROUPS[gi][1]
        rows = tm // dil
        for r in range(dil):
            piece = jnp.concatenate([slab_scr[s, pl.ds(r, rows, stride=dil), :] for s in range(n_slabs)], axis=1)
            hp_scr[r * rows:(r + 1) * rows, :] = piece.astype(BF16)
        c0 = N_GATES + n_hg + gi * N_QKV
        res = _dot(hp_scr[...], w_ref[:, c0:c0 + N_QKV]).astype(out_ref.dtype)
        for r in range(dil):
            out_ref[0, r] = res[r * rows:(r + 1) * rows, :]

    matmul_pieces = [lambda: gate_columns(0), lambda: gate_columns(1024), group0,
                     lambda: dilated_group(1, a1_ref, hp1_scr), lambda: dilated_group(2, a2_ref, hp2_scr)]
    lb = _hgrn_lower_bound(lbl_ref, layer)
    gain = hgain_ref[...]
    lvl = lvl_ref[...]
    c = HG_CHUNK
    n_chunks = tm // c
    for k in range(n_chunks):
        if k < len(matmul_pieces):
            matmul_pieces[k]()
        rows = slice(k * c, (k + 1) * c)
        ya_ref[0, rows] = _hgrn_chunk(hg_scr[rows, 0:HG_WIDTH], hg_scr[rows, HG_WIDTH:2 * HG_WIDTH],
                                      hg_scr[rows, 2 * HG_WIDTH:3 * HG_WIDTH], hg_scr[rows, 3 * HG_WIDTH:n_hg],
                                      lb, gain, lvl, st_scr, g_scr).astype(ya_ref.dtype)
    for piece in matmul_pieces[n_chunks:]:
        piece()


def _in_projection(x, gain, sc, sh, w, lb_logits, hg_gain, layer):
    bsz, seq, d = x.shape
    tm = 512
    dils = [dil for _, dil in ATT_GROUPS]
    assert dils[0] == 1 and all(tm % (16 * dil) == 0 for dil in dils) and tm % HG_CHUNK == 0
    lvl = jnp.asarray(_hgrn_level_table(HG_CHUNK))

    def full(shape):
        return pl.BlockSpec(shape, lambda b, i: (0,) * len(shape), pipeline_mode=pl.Buffered(1))

    out_shape = [jax.ShapeDtypeStruct((bsz, seq, N_GATES), BF16)]
    out_specs = [pl.BlockSpec((1, tm, N_GATES), lambda b, i: (b, i, 0))]
    for dil in dils:
        out_shape.append(jax.ShapeDtypeStruct((bsz, dil, seq // dil, N_QKV), BF16))
        out_specs.append(pl.BlockSpec((1, dil, tm // dil, N_QKV), lambda b, i: (b, 0, i, 0)))
    out_shape.append(jax.ShapeDtypeStruct((bsz, seq, HG_WIDTH), BF16))
    out_specs.append(pl.BlockSpec((1, tm, HG_WIDTH), lambda b, i: (b, i, 0)))
    return pl.pallas_call(
        functools.partial(_inproj_kernel, layer=layer),
        grid=(bsz, seq // tm),
        in_specs=[pl.BlockSpec((1, tm, d), lambda b, i: (b, i, 0)),
                  full((1, d)),
                  pl.BlockSpec((1, 1, d), lambda b, i: (b, 0, 0)),
                  pl.BlockSpec((1, 1, d), lambda b, i: (b, 0, 0)),
                  full(w.shape), full(lb_logits.shape), full((1, HG_WIDTH)), full(lvl.shape)],
        out_specs=out_specs,
        out_shape=out_shape,
        scratch_shapes=[pltpu.VMEM((tm, 4 * HG_WIDTH), BF16),
                        pltpu.VMEM((d // LANES, tm, LANES), F32),
                        pltpu.VMEM((tm, d), BF16), pltpu.VMEM((tm, d), BF16),
                        pltpu.VMEM((HG_HEADS, HG_DIM, HG_DIM), F32),
                        pltpu.VMEM((HG_CHUNK, HG_WIDTH), F32)],
        compiler_params=_params("arbitrary", "arbitrary"),
        name="in_projection_hgrn2",
    )(x, gain.reshape(1, d), sc, sh, w, lb_logits, hg_gain.reshape(1, HG_WIDTH), lvl)


def _hgrn_level_table(c):
    t = np.arange(c)[:, None]
    s = np.arange(c)[None, :]
    x = t ^ s
    lvl = np.where(x > 0, np.floor(np.log2(np.maximum(x, 1))).astype(np.int32), 0)
    n_levels = int(np.log2(c))
    out = np.where(t > s, lvl, np.where(t == s, n_levels, -1)).astype(np.int32)
    return out


def _hgrn_lower_bound(lbl_ref, layer):
    a = lbl_ref[...]
    e = jnp.exp(a - jnp.max(a, axis=0, keepdims=True))
    return jnp.sum(e[:layer + 1], axis=0, keepdims=True) / jnp.sum(e, axis=0, keepdims=True)


def _hgrn_chunk(q_raw, f_raw, v, gate_raw, lb, gain, lvl, st_scr, g_scr):
    c = q_raw.shape[0]
    n_levels = c.bit_length() - 1

    q = _silu(q_raw.astype(F32))
    fg = lb + (1.0 - lb) * _sigmoid(f_raw.astype(F32))
    kk = 1.0 - fg
    logf = jnp.log(fg) * LOG2_E

    r2 = lax.broadcasted_iota(jnp.int32, (c, c), 0)
    c2 = lax.broadcasted_iota(jnp.int32, (c, c), 1)
    tril = jnp.where(r2 >= c2, 1.0, 0.0).astype(BF16)
    hi = logf.astype(BF16)
    lo = (logf - hi.astype(F32)).astype(BF16)
    g = _dot(tril, hi) + _dot(tril, lo)
    g_scr[...] = g

    row = lax.broadcasted_iota(jnp.int32, (c, HG_DIM), 0)
    lane_tiles = [slice(hd * HG_DIM, (hd + 1) * HG_DIM) for hd in range(HG_HEADS)]
    exps, rowbs = [], []
    for li in range(n_levels):
        h = 1 << li
        if h >= 4:
            parts = [jnp.broadcast_to(g_scr[pl.ds(p * 2 * h + h - 1, 1), :], (2 * h, HG_WIDTH))
                     for p in range(c // (2 * h))]
            e_l = -jnp.abs(g - jnp.concatenate(parts, axis=0))
            rowb = None if h >= 8 else (row & (2 * h - 1)) >= h
        elif h == 2:
            up = pltpu.roll(logf, 1, 0)
            dn = pltpu.roll(logf, c - 1, 0)
            r4 = row & 3
            m0, m1, m2 = r4 == 0, r4 == 1, r4 == 2
            e_l = jnp.concatenate(
                [jnp.where(m0, dn[:, sl], jnp.where(m1, 0.0, jnp.where(m2, logf[:, sl], up[:, sl] + logf[:, sl])))
                 for sl in lane_tiles], axis=1)
            rowb = r4 >= 2
        else:
            rowb = (row & 1) == 1
            e_l = jnp.concatenate([jnp.where(rowb, logf[:, sl], 0.0) for sl in lane_tiles], axis=1)
        exps.append(jnp.exp2(e_l))
        rowbs.append(rowb)

    eg = jnp.exp2(g)
    g_last = g_scr[pl.ds(c - 1, 1), :]
    e_last = jnp.exp2(g_last)
    kdec = kk * jnp.exp2(g_last - g)
    gate = _silu(gate_raw.astype(F32))

    def query_or_key_rows(li, qh, kh):
        h = 1 << li
        if rowbs[li] is not None:
            return jnp.where(rowbs[li], qh, kh)
        blocks = []
        for p in range(c // (2 * h)):
            blocks += [kh[2 * p * h:(2 * p + 1) * h], qh[(2 * p + 1) * h:(2 * p + 2) * h]]
        return jnp.concatenate(blocks, axis=0)

    outs = []
    for hd in range(HG_HEADS):
        sl = lane_tiles[hd]
        qh, kh = q[:, sl], kk[:, sl]
        scores = jnp.where(lvl == n_levels, _dot_nt(qh.astype(BF16), kh.astype(BF16)), 0.0)
        for li in range(n_levels):
            xl = (query_or_key_rows(li, qh, kh) * exps[li][:, sl]).astype(BF16)
            scores = jnp.where(lvl == li, _dot_nt(xl, xl), scores)
        vh = v[:, sl]
        st = st_scr[hd]
        o = _dot(scores.astype(BF16), vh)
        o = o + _dot_nt((qh * eg[:, sl]).astype(BF16), st.astype(BF16))
        st_scr[hd] = st * e_last[:, sl] + _dot_tn(vh, kdec[:, sl].astype(BF16))
        ms = jnp.mean(o * o, axis=-1, keepdims=True)
        outs.append(o * lax.rsqrt(ms + EPS) * gain[:, sl] * gate[:, sl])
    return jnp.concatenate(outs, axis=1)


def _attn_bias_tables(group):
    window, dil = ATT_GROUPS[group]
    band = window // dil
    blk = ATT_BLOCK
    heads = np.arange(group * ATT_HPG, (group + 1) * ATT_HPG) + 1
    slopes = np.exp2(-ALIBI_MAX * heads / N_ATT_HEADS).astype(np.float32)
    qi = np.arange(blk)[:, None] + blk
    ki = np.arange(2 * blk)[None, :]
    dist = qi - ki
    in_band = (dist >= 0) & (dist <= band)
    bias = -(slopes[:, None, None] * (dist.astype(np.float32) * dil)[None]) * np.float32(LOG2_E)
    tables = [np.where(in_band[None], bias, -np.inf), np.where((in_band & (ki >= blk))[None], bias, -np.inf)]
    return np.stack(tables).astype(np.float32)


def _attn_kernel(q_ref, kp_ref, kc_ref, vp_ref, vc_ref, bias_ref, o_ref, lse_ref, kbuf, vbuf, o_scr, *, dil):
    blk = ATT_BLOCK
    n_mb = q_ref.shape[2] // blk
    step = pl.program_id(1)
    kbuf[:, 0:blk, :] = kp_ref[0]
    kbuf[:, blk:, :] = kc_ref[0]
    vbuf[:, 0:blk, :] = vp_ref[0]
    vbuf[:, blk:, :] = vc_ref[0]

    def body(it, carry):
        mb = it // dil
        r = it - mb * dil
        row0 = pl.multiple_of(mb * blk, blk)
        table = jnp.where((step > 0) | (mb > 0), 0, 1)
        tok0 = mb * (blk * dil) + r
        rows = pl.ds(tok0, blk, stride=dil) if dil > 1 else pl.ds(tok0, blk)
        ones = jnp.ones((2 * blk, ATT_E), BF16)
        for j in range(ATT_HPG):
            sl = slice(j * ATT_E, (j + 1) * ATT_E)
            q = q_ref[0, r, pl.ds(row0, blk), sl]
            kw = kbuf[r, pl.ds(row0, 2 * blk), sl]
            vw = jnp.concatenate([vbuf[r, pl.ds(row0, 2 * blk), sl], ones], axis=1)
            s = _dot_nt(q, kw) * (ATT_E ** -0.5 * LOG2_E) + bias_ref[table, j]
            m = jnp.max(jnp.maximum(s[:, :blk], s[:, blk:]), axis=-1, keepdims=True)
            p = jnp.exp2(s - m)
            ol = _dot(p.astype(BF16), vw)
            l = ol[:, ATT_E:]
            o_scr[j, rows, :] = ol[:, :ATT_E] / l
            lse_ref[0, j, rows, :] = m + jnp.log(l) * LOG2_E
        return carry

    lax.fori_loop(0, n_mb * dil, body, 0, unroll=ATT_UNROLL)
    for j in range(ATT_HPG):
        o_ref[0, :, j * ATT_E:(j + 1) * ATT_E] = o_scr[j].astype(o_ref.dtype)


def _dilated_group(qkv, group):
    bsz, d, lsub, _ = qkv.shape
    seq = d * lsub
    window, dil = ATT_GROUPS[group]
    band = window // dil
    blk = ATT_BLOCK
    assert dil == d and band <= blk and ATT_SUPER % (blk * d) == 0 and seq % ATT_SUPER == 0
    rows = ATT_SUPER // d
    bias = jnp.asarray(_attn_bias_tables(group))

    def cur(col):
        return pl.BlockSpec((1, d, rows, ATT_GW), lambda b, n: (b, 0, n, col))

    def prev(col):
        return pl.BlockSpec((1, d, blk, ATT_GW), lambda b, n: (b, 0, jnp.maximum(n * (rows // blk) - 1, 0), col))

    return pl.pallas_call(
        functools.partial(_attn_kernel, dil=d),
        grid=(bsz, seq // ATT_SUPER),
        in_specs=[cur(0), prev(1), cur(1), prev(2), cur(2),
                  pl.BlockSpec(bias.shape, lambda b, n: (0, 0, 0, 0), pipeline_mode=pl.Buffered(1))],
        out_specs=[pl.BlockSpec((1, ATT_SUPER, ATT_GW), lambda b, n: (b, n, 0)),
                   pl.BlockSpec((1, ATT_HPG, ATT_SUPER, ATT_E), lambda b, n: (b, 0, n, 0))],
        out_shape=[jax.ShapeDtypeStruct((bsz, seq, ATT_GW), BF16),
                   jax.ShapeDtypeStruct((bsz, ATT_HPG, seq, ATT_E), F32)],
        scratch_shapes=[pltpu.VMEM((d, rows + blk, ATT_GW), BF16), pltpu.VMEM((d, rows + blk, ATT_GW), BF16),
                        pltpu.VMEM((ATT_HPG, ATT_SUPER, ATT_E), F32)],
        compiler_params=_params("arbitrary", "arbitrary"),
        name=f"dilated_attn_g{group}",
    )(qkv, qkv, qkv, qkv, qkv, bias)


def _mix_kernel(ya_ref, o1_ref, o2_ref, o3_ref, l1_ref, l2_ref, l3_ref, ga_ref, gb_ref, x_ref,
                wa_ref, wb_ref, wo_ref, gt_ref, sc_ref, sh_ref, gn_ref, x1_ref, h2_ref):
    parts = []
    for j in range(ATT_HPG):
        sl = slice(j * ATT_E, (j + 1) * ATT_E)
        l1, l2, l3 = l1_ref[0, j], l2_ref[0, j], l3_ref[0, j]
        m = jnp.maximum(jnp.maximum(l1, l2), l3)
        e1, e2, e3 = jnp.exp2(l1 - m), jnp.exp2(l2 - m), jnp.exp2(l3 - m)
        num = (e1 * o1_ref[0, :, sl].astype(F32) + e2 * o2_ref[0, :, sl].astype(F32)
               + e3 * o3_ref[0, :, sl].astype(F32))
        parts.append(num / (e1 + e2 + e3))
    yb = jnp.concatenate(parts, axis=1)
    za = _dot(ya_ref[0], wa_ref[...])
    zb = _dot(yb.astype(BF16), wb_ref[...])
    ta = jnp.tanh((0.5 * ga_ref[0]).astype(F32))
    tb = jnp.tanh((0.5 * gb_ref[0]).astype(F32))
    merged2 = (ta + 1.0) * za + (tb + 1.0) * zb
    x1 = x_ref[0] + (0.5 * (1.0 + gt_ref[0])) * _dot(merged2.astype(BF16), wo_ref[...])
    x1_ref[0] = x1
    ms = jnp.mean(x1 * x1, axis=-1, keepdims=True)
    scale = gn_ref[...] * (1.0 + sc_ref[0])
    h2_ref[0] = (x1 * lax.rsqrt(ms + EPS) * scale + sh_ref[0]).astype(h2_ref.dtype)


def _mix(ya, outs, lses, main, x, wa, wb, wo, gt, sc, sh, gn):
    bsz, seq, d = x.shape
    tm = 512

    def tok(width, colb=0):
        return pl.BlockSpec((1, tm, width), lambda b, i, colb=colb: (b, i, colb))

    def full(shape):
        return pl.BlockSpec(shape, lambda b, i: (0,) * len(shape), pipeline_mode=pl.Buffered(1))

    def per_batch():
        return pl.BlockSpec((1, 1, d), lambda b, i: (b, 0, 0))

    lse_spec = pl.BlockSpec((1, ATT_HPG, tm, ATT_E), lambda b, i: (b, 0, i, 0))
    return pl.pallas_call(
        _mix_kernel,
        grid=(bsz, seq // tm),
        in_specs=[tok(HG_WIDTH)] + [tok(ATT_GW)] * 3 + [lse_spec] * 3 + [tok(d, 0), tok(d, 1), tok(d),
                  full(wa.shape), full(wb.shape), full(wo.shape),
                  per_batch(), per_batch(), per_batch(), full((1, d))],
        out_specs=[tok(d), tok(d)],
        out_shape=[jax.ShapeDtypeStruct((bsz, seq, d), F32), jax.ShapeDtypeStruct((bsz, seq, d), BF16)],
        compiler_params=_params("arbitrary", "arbitrary"),
        name="mix",
    )(ya, *outs, *lses, main, main, x, wa, wb, wo, gt, sc, sh, gn.reshape(1, d))


def _ffn_kernel(h_ref, x_ref, wu_ref, wd_ref, cw_ref, cb_ref, gt_ref, gf_ref, o_ref, carry_scr, ug_scr, uv_scr,
                act_scr, *, final_norm):
    tm = h_ref.shape[1]
    fc = FFN_CHUNK

    @pl.when(pl.program_id(1) == 0)
    def _():
        carry_scr[...] = jnp.zeros_like(carry_scr)

    h = h_ref[0]
    row = lax.broadcasted_iota(jnp.int32, (tm, fc), 0)
    n_chunks = D_FF // fc

    def up_project(ci):
        ug_scr[ci % 2] = _dot(h, wu_ref[:, ci * fc:(ci + 1) * fc])
        uv_scr[ci % 2] = _dot(h, wu_ref[:, D_FF + ci * fc:D_FF + (ci + 1) * fc])

    up_project(0)
    for ci in range(n_chunks):
        if ci + 1 < n_chunks:
            up_project(ci + 1)
        cs = slice(ci * fc, (ci + 1) * fc)
        ug = ug_scr[ci % 2]
        prev2 = jnp.broadcast_to(carry_scr[6:7, cs], (tm, fc))
        prev1 = jnp.broadcast_to(carry_scr[7:8, cs], (tm, fc))
        r1 = pltpu.roll(ug, 1, 0)
        r2 = pltpu.roll(ug, 2, 0)
        u1 = jnp.where(row == 0, prev1, r1)
        u2 = jnp.where(row == 0, prev2, jnp.where(row == 1, prev1, r2))
        carry_scr[:, cs] = ug[tm - 8:tm, :]
        y = cb_ref[:, cs] + u2 * cw_ref[0:1, cs] + u1 * cw_ref[1:2, cs] + ug * cw_ref[2:3, cs]
        gel = 0.5 * y * (1.0 + jnp.tanh(0.7978845608028654 * (y + 0.044715 * (y * y * y))))
        act_scr[:, cs] = (gel * uv_scr[ci % 2]).astype(BF16)
    x2 = x_ref[0] + (1.0 + gt_ref[0]) * _dot(act_scr[...], wd_ref[...])
    if final_norm:
        ms = jnp.mean(x2 * x2, axis=-1, keepdims=True)
        x2 = x2 * lax.rsqrt(ms + EPS) * gf_ref[...]
    o_ref[0] = x2


def _ffn(h2, x1, wu, wd, cw, cb, gt, gf, final_norm):
    bsz, seq, d = x1.shape
    tm = 1024

    def tok():
        return pl.BlockSpec((1, tm, d), lambda b, i: (b, i, 0))

    def full(shape):
        return pl.BlockSpec(shape, lambda b, i: (0,) * len(shape), pipeline_mode=pl.Buffered(1))

    return pl.pallas_call(
        functools.partial(_ffn_kernel, final_norm=final_norm),
        grid=(bsz, seq // tm),
        in_specs=[tok(), tok(), full(wu.shape), full(wd.shape), full(cw.shape), full((1, D_FF)),
                  pl.BlockSpec((1, 1, d), lambda b, i: (b, 0, 0)), full((1, d))],
        out_specs=tok(),
        out_shape=jax.ShapeDtypeStruct((bsz, seq, d), F32),
        scratch_shapes=[pltpu.VMEM((8, D_FF), F32), pltpu.VMEM((2, tm, FFN_CHUNK), F32),
                        pltpu.VMEM((2, tm, FFN_CHUNK), F32), pltpu.VMEM((tm, D_FF), BF16)],
        compiler_params=_params("arbitrary", "arbitrary"),
        name="ffn",
    )(h2, x1, wu, wd, cw, cb.reshape(1, D_FF), gt, gf.reshape(1, d))


def kernel(x, c, w_ada, b_ada, g_norm_mix, w_in, lb_logits, g_hg_norm, w_branch_a, w_branch_b, w_out, g_norm_ffn,
           w_up, conv_w, conv_b, w_down, g_final):
    depth = w_ada.shape[0]
    bsz = x.shape[0]
    n_hg = 4 * HG_WIDTH
    for l in range(depth):
        mod = _modulation(c, w_ada[l], b_ada[l]).reshape(bsz, N_MOD, 1, D_MODEL)
        sh1, sc1, gt1, sh2, sc2, gt2 = [mod[:, k] for k in range(N_MOD)]
        w = w_in[l]
        n_grp = len(ATT_GROUPS)
        o_q = n_hg
        o_g = n_hg + 3 * n_grp * ATT_GW
        cols = [w[:, o_g:], w[:, :n_hg]]
        for g in range(n_grp):
            cols += [w[:, o_q + (t * n_grp + g) * ATT_GW: o_q + (t * n_grp + g + 1) * ATT_GW] for t in range(3)]
        w_perm = jnp.concatenate(cols, axis=1).astype(BF16)

        gates, *qkvs, ya = _in_projection(x, g_norm_mix[l], sc1, sh1, w_perm, lb_logits,
                                          g_hg_norm[l].reshape(HG_WIDTH), l)
        groups = [_dilated_group(qkvs[g], g) for g in range(n_grp)]
        x1, h2 = _mix(ya, [o for o, _ in groups], [s for _, s in groups], gates, x,
                      w_branch_a[l].astype(BF16), w_branch_b[l].astype(BF16), w_out[l].astype(BF16),
                      gt1, sc2, sh2, g_norm_ffn[l])
        x = _ffn(h2, x1, w_up[l].astype(BF16), w_down[l].astype(BF16), conv_w[l], conv_b[l], gt2, g_final,
                 final_norm=(l == depth - 1))
    return x
```

```python
import functools

import numpy as np
import jax
import jax.numpy as jnp
from jax import lax
from jax.experimental import pallas as pl
from jax.experimental.pallas import tpu as pltpu

F32 = jnp.float32
BF16 = jnp.bfloat16

D_MODEL = 1024
HG_HEADS = 4
HG_DIM = 128
HG_WIDTH = HG_HEADS * HG_DIM
ATT_GROUPS = ((128, 1), (512, 4), (2048, 16))
ATT_HPG = 4
ATT_E = 128
N_ATT_HEADS = ATT_HPG * len(ATT_GROUPS)
ATT_GW = ATT_HPG * ATT_E
ATT_BLOCK = 128
ALIBI_MAX = 8.0
D_FF = 2816
N_MOD = 6
EPS = 1e-6
LOG2_E = 1.4426950408889634

N_GATES = 2 * D_MODEL
N_QKV = 3 * ATT_GW
ATT_SUPER = 2048
ATT_UNROLL = 16
LANES = 128

VMEM_LIMIT_BYTES = 56 * 1024 * 1024

HG_CHUNK = 256
FFN_CHUNK = 256


def _sigmoid(x):
    return 1.0 / (1.0 + jnp.exp(-x))


def _gate_sigmoid(x):
    return 0.5 * jnp.tanh(0.5 * x) + 0.5


def _silu(x):
    return x * _gate_sigmoid(x)


def _dot(a, b):
    return jnp.dot(a, b, preferred_element_type=F32)


def _dot_nt(a, b):
    return lax.dot_general(a, b, (((1,), (1,)), ((), ())), preferred_element_type=F32)


def _dot_tn(a, b):
    return lax.dot_general(a, b, (((0,), (0,)), ((), ())), preferred_element_type=F32)


def _params(*sem):
    return pltpu.CompilerParams(dimension_semantics=sem, vmem_limit_bytes=VMEM_LIMIT_BYTES)


def _mod_kernel(c_ref, w_ref, b_ref, o_ref):
    s = _silu(c_ref[...]).astype(BF16)
    o_ref[...] = _dot(s, w_ref[...].astype(BF16)) + b_ref[...]


def _modulation(c, w, b):
    bsz, d = c.shape
    n = w.shape[1]
    tn = 1024
    return pl.pallas_call(
        _mod_kernel,
        grid=(n // tn,),
        in_specs=[pl.BlockSpec((bsz, d), lambda j: (0, 0)),
                  pl.BlockSpec((d, tn), lambda j: (0, j)),
                  pl.BlockSpec((1, tn), lambda j: (0, j))],
        out_specs=pl.BlockSpec((bsz, tn), lambda j: (0, j)),
        out_shape=jax.ShapeDtypeStruct((bsz, n), F32),
        compiler_params=_params("arbitrary"),
        name="modulation",
    )(c, w, b.reshape(1, n))


def _inproj_kernel(x_ref, g_ref, sc_ref, sh_ref, w_ref, lbl_ref, hgain_ref, lvl_ref,
                   gates_ref, a0_ref, a1_ref, a2_ref, ya_ref,
                   hg_scr, slab_scr, hp1_scr, hp2_scr, st_scr, g_scr, *, layer):
    tm, d_model = x_ref.shape[1], x_ref.shape[2]

    @pl.when(pl.program_id(1) == 0)
    def _():
        st_scr[...] = jnp.zeros_like(st_scr)

    x = x_ref[0]
    ms = jnp.mean(x * x, axis=-1, keepdims=True)
    scale = g_ref[...] * (1.0 + sc_ref[0])
    h = x * lax.rsqrt(ms + EPS) * scale + sh_ref[0]
    hb = h.astype(BF16)
    n_slabs = d_model // LANES
    for s in range(n_slabs):
        slab_scr[s] = h[:, s * LANES:(s + 1) * LANES]

    n_hg = 4 * HG_WIDTH
    for c0 in range(0, n_hg, 1024):
        hg_scr[:, c0:c0 + 1024] = _dot(hb, w_ref[:, N_GATES + c0:N_GATES + c0 + 1024]).astype(BF16)

    def gate_columns(c0):
        gates_ref[0, :, c0:c0 + 1024] = _dot(hb, w_ref[:, c0:c0 + 1024]).astype(gates_ref.dtype)

    def group0():
        c0 = N_GATES + n_hg
        a0_ref[0, 0] = _dot(hb, w_ref[:, c0:c0 + N_QKV]).astype(a0_ref.dtype)

    def dilated_group(gi, out_ref, hp_scr):
        dil = ATT_GROUPS[gi][1]
        rows = tm // dil
        for r in range(dil):
            piece = jnp.concatenate([slab_scr[s, pl.ds(r, rows, stride=dil), :] for s in range(n_slabs)], axis=1)
            hp_scr[r * rows:(r + 1) * rows, :] = piece.astype(BF16)
        c0 = N_GATES + n_hg + gi * N_QKV
        res = _dot(hp_scr[...], w_ref[:, c0:c0 + N_QKV]).astype(out_ref.dtype)
        for r in range(dil):
            out_ref[0, r] = res[r * rows:(r + 1) * rows, :]

    matmul_pieces = [lambda: gate_columns(0), lambda: gate_columns(1024), group0,
                     lambda: dilated_group(1, a1_ref, hp1_scr), lambda: dilated_group(2, a2_ref, hp2_scr)]
    lb = _hgrn_lower_bound(lbl_ref, layer)
    gain = hgain_ref[...]
    lvl = lvl_ref[...]
    c = HG_CHUNK
    n_chunks = tm // c
    for k in range(n_chunks):
        if k < len(matmul_pieces):
            matmul_pieces[k]()
        rows = slice(k * c, (k + 1) * c)
        ya_ref[0, rows] = _hgrn_chunk(hg_scr[rows, 0:HG_WIDTH], hg_scr[rows, HG_WIDTH:2 * HG_WIDTH],
                                      hg_scr[rows, 2 * HG_WIDTH:3 * HG_WIDTH], hg_scr[rows, 3 * HG_WIDTH:n_hg],
                                      lb, gain, lvl, st_scr, g_scr).astype(ya_ref.dtype)
    for piece in matmul_pieces[n_chunks:]:
        piece()


def _in_projection(x, gain, sc, sh, w, lb_logits, hg_gain, layer):
    bsz, seq, d = x.shape
    tm = 512
    dils = [dil for _, dil in ATT_GROUPS]
    assert dils[0] == 1 and all(tm % (16 * dil) == 0 for dil in dils) and tm % HG_CHUNK == 0
    lvl = jnp.asarray(_hgrn_level_table(HG_CHUNK))

    def full(shape):
        return pl.BlockSpec(shape, lambda b, i: (0,) * len(shape), pipeline_mode=pl.Buffered(1))

    out_shape = [jax.ShapeDtypeStruct((bsz, seq, N_GATES), BF16)]
    out_specs = [pl.BlockSpec((1, tm, N_GATES), lambda b, i: (b, i, 0))]
    for dil in dils:
        out_shape.append(jax.ShapeDtypeStruct((bsz, dil, seq // dil, N_QKV), BF16))
        out_specs.append(pl.BlockSpec((1, dil, tm // dil, N_QKV), lambda b, i: (b, 0, i, 0)))
    out_shape.append(jax.ShapeDtypeStruct((bsz, seq, HG_WIDTH), BF16))
    out_specs.append(pl.BlockSpec((1, tm, HG_WIDTH), lambda b, i: (b, i, 0)))
    return pl.pallas_call(
        functools.partial(_inproj_kernel, layer=layer),
        grid=(bsz, seq // tm),
        in_specs=[pl.BlockSpec((1, tm, d), lambda b, i: (b, i, 0)),
                  full((1, d)),
                  pl.BlockSpec((1, 1, d), lambda b, i: (b, 0, 0)),
                  pl.BlockSpec((1, 1, d), lambda b, i: (b, 0, 0)),
                  full(w.shape), full(lb_logits.shape), full((1, HG_WIDTH)), full(lvl.shape)],
        out_specs=out_specs,
        out_shape=out_shape,
        scratch_shapes=[pltpu.VMEM((tm, 4 * HG_WIDTH), BF16),
                        pltpu.VMEM((d // LANES, tm, LANES), F32),
                        pltpu.VMEM((tm, d), BF16), pltpu.VMEM((tm, d), BF16),
                        pltpu.VMEM((HG_HEADS, HG_DIM, HG_DIM), F32),
                        pltpu.VMEM((HG_CHUNK, HG_WIDTH), F32)],
        compiler_params=_params("arbitrary", "arbitrary"),
        name="in_projection_hgrn2",
    )(x, gain.reshape(1, d), sc, sh, w, lb_logits, hg_gain.reshape(1, HG_WIDTH), lvl)


def _hgrn_level_table(c):
    t = np.arange(c)[:, None]
    s = np.arange(c)[None, :]
    x = t ^ s
    lvl = np.where(x > 0, np.floor(np.log2(np.maximum(x, 1))).astype(np.int32), 0)
    n_levels = int(np.log2(c))
    out = np.where(t > s, lvl, np.where(t == s, n_levels, -1)).astype(np.int32)
    return out


def _hgrn_lower_bound(lbl_ref, layer):
    a = lbl_ref[...]
    e = jnp.exp(a - jnp.max(a, axis=0, keepdims=True))
    return jnp.sum(e[:layer + 1], axis=0, keepdims=True) / jnp.sum(e, axis=0, keepdims=True)


def _hgrn_chunk(q_raw, f_raw, v, gate_raw, lb, gain, lvl, st_scr, g_scr):
    c = q_raw.shape[0]
    n_levels = c.bit_length() - 1

    q = _silu(q_raw.astype(F32))
    fg = lb + (1.0 - lb) * _sigmoid(f_raw.astype(F32))
    kk = 1.0 - fg
    logf = jnp.log(fg) * LOG2_E

    r2 = lax.broadcasted_iota(jnp.int32, (c, c), 0)
    c2 = lax.broadcasted_iota(jnp.int32, (c, c), 1)
    tril = jnp.where(r2 >= c2, 1.0, 0.0).astype(BF16)
    hi = logf.astype(BF16)
    lo = (logf - hi.astype(F32)).astype(BF16)
    g = _dot(tril, hi) + _dot(tril, lo)
    g_scr[...] = g

    row = lax.broadcasted_iota(jnp.int32, (c, HG_DIM), 0)
    lane_tiles = [slice(hd * HG_DIM, (hd + 1) * HG_DIM) for hd in range(HG_HEADS)]
    exps, rowbs = [], []
    for li in range(n_levels):
        h = 1 << li
        if h >= 4:
            parts = [jnp.broadcast_to(g_scr[pl.ds(p * 2 * h + h - 1, 1), :], (2 * h, HG_WIDTH))
                     for p in range(c // (2 * h))]
            e_l = -jnp.abs(g - jnp.concatenate(parts, axis=0))
            rowb = None if h >= 8 else (row & (2 * h - 1)) >= h
        elif h == 2:
            up = pltpu.roll(logf, 1, 0)
            dn = pltpu.roll(logf, c - 1, 0)
            r4 = row & 3
            m0, m1, m2 = r4 == 0, r4 == 1, r4 == 2
            e_l = jnp.concatenate(
                [jnp.where(m0, dn[:, sl], jnp.where(m1, 0.0, jnp.where(m2, logf[:, sl], up[:, sl] + logf[:, sl])))
                 for sl in lane_tiles], axis=1)
            rowb = r4 >= 2
        else:
            rowb = (row & 1) == 1
            e_l = jnp.concatenate([jnp.where(rowb, logf[:, sl], 0.0) for sl in lane_tiles], axis=1)
        exps.append(jnp.exp2(e_l))
        rowbs.append(rowb)

    eg = jnp.exp2(g)
    g_last = g_scr[pl.ds(c - 1, 1), :]
    e_last = jnp.exp2(g_last)
    kdec = kk * jnp.exp2(g_last - g)
    gate = _silu(gate_raw.astype(F32))

    def query_or_key_rows(li, qh, kh):
        h = 1 << li
        if rowbs[li] is not None:
            return jnp.where(rowbs[li], qh, kh)
        blocks = []
        for p in range(c // (2 * h)):
            blocks += [kh[2 * p * h:(2 * p + 1) * h], qh[(2 * p + 1) * h:(2 * p + 2) * h]]
        return jnp.concatenate(blocks, axis=0)

    outs = []
    for hd in range(HG_HEADS):
        sl = lane_tiles[hd]
        qh, kh = q[:, sl], kk[:, sl]
        scores = jnp.where(lvl == n_levels, _dot_nt(qh.astype(BF16), kh.astype(BF16)), 0.0)
        for li in range(n_levels):
            xl = (query_or_key_rows(li, qh, kh) * exps[li][:, sl]).astype(BF16)
            scores = jnp.where(lvl == li, _dot_nt(xl, xl), scores)
        vh = v[:, sl]
        st = st_scr[hd]
        o = _dot(scores.astype(BF16), vh)
        o = o + _dot_nt((qh * eg[:, sl]).astype(BF16), st.astype(BF16))
        st_scr[hd] = st * e_last[:, sl] + _dot_tn(vh, kdec[:, sl].astype(BF16))
        ms = jnp.mean(o * o, axis=-1, keepdims=True)
        outs.append(o * lax.rsqrt(ms + EPS) * gain[:, sl] * gate[:, sl])
    return jnp.concatenate(outs, axis=1)


def _attn_bias_tables(group):
    window, dil = ATT_GROUPS[group]
    band = window // dil
    blk = ATT_BLOCK
    heads = np.arange(group * ATT_HPG, (group + 1) * ATT_HPG) + 1
    slopes = np.exp2(-ALIBI_MAX * heads / N_ATT_HEADS).astype(np.float32)
    qi = np.arange(blk)[:, None] + blk
    ki = np.arange(2 * blk)[None, :]
    dist = qi - ki
    in_band = (dist >= 0) & (dist <= band)
    bias = -(slopes[:, None, None] * (dist.astype(np.float32) * dil)[None]) * np.float32(LOG2_E)
    tables = [np.where(in_band[None], bias, -np.inf), np.where((in_band & (ki >= blk))[None], bias, -np.inf)]
    return np.stack(tables).astype(np.float32)


def _attn_kernel(q_ref, kp_ref, kc_ref, vp_ref, vc_ref, bias_ref, o_ref, lse_ref, kbuf, vbuf, o_scr, *, dil):
    blk = ATT_BLOCK
    n_mb = q_ref.shape[2] // blk
    step = pl.program_id(1)
    kbuf[:, 0:blk, :] = kp_ref[0]
    kbuf[:, blk:, :] = kc_ref[0]
    vbuf[:, 0:blk, :] = vp_ref[0]
    vbuf[:, blk:, :] = vc_ref[0]

    def body(it, carry):
        mb = it // dil
        r = it - mb * dil
        row0 = pl.multiple_of(mb * blk, blk)
        table = jnp.where((step > 0) | (mb > 0), 0, 1)
        tok0 = mb * (blk * dil) + r
        rows = pl.ds(tok0, blk, stride=dil) if dil > 1 else pl.ds(tok0, blk)
        ones = jnp.ones((2 * blk, ATT_E), BF16)
        for j in range(ATT_HPG):
            sl = slice(j * ATT_E, (j + 1) * ATT_E)
            q = q_ref[0, r, pl.ds(row0, blk), sl]
            kw = kbuf[r, pl.ds(row0, 2 * blk), sl]
            vw = jnp.concatenate([vbuf[r, pl.ds(row0, 2 * blk), sl], ones], axis=1)
            s = _dot_nt(q, kw) * (ATT_E ** -0.5 * LOG2_E) + bias_ref[table, j]
            m = jnp.max(jnp.maximum(s[:, :blk], s[:, blk:]), axis=-1, keepdims=True)
            p = jnp.exp2(s - m)
            ol = _dot(p.astype(BF16), vw)
            l = ol[:, ATT_E:]
            if dil > 1:
                o_scr[j, rows, :] = ol[:, :ATT_E] / l
            else:
                o_ref[0, rows, sl] = (ol[:, :ATT_E] / l).astype(o_ref.dtype)
            lse_ref[0, j, rows, :] = m + jnp.log(l) * LOG2_E
        return carry

    lax.fori_loop(0, n_mb * dil, body, 0, unroll=ATT_UNROLL)
    if dil > 1:
        for j in range(ATT_HPG):
            o_ref[0, :, j * ATT_E:(j + 1) * ATT_E] = o_scr[j].astype(o_ref.dtype)


def _dilated_group(qkv, group):
    bsz, d, lsub, _ = qkv.shape
    seq = d * lsub
    window, dil = ATT_GROUPS[group]
    band = window // dil
    blk = ATT_BLOCK
    assert dil == d and band <= blk and ATT_SUPER % (blk * d) == 0 and seq % ATT_SUPER == 0
    rows = ATT_SUPER // d
    bias = jnp.asarray(_attn_bias_tables(group))

    def cur(col):
        return pl.BlockSpec((1, d, rows, ATT_GW), lambda b, n: (b, 0, n, col))

    def prev(col):
        return pl.BlockSpec((1, d, blk, ATT_GW), lambda b, n: (b, 0, jnp.maximum(n * (rows // blk) - 1, 0), col))

    return pl.pallas_call(
        functools.partial(_attn_kernel, dil=d),
        grid=(bsz, seq // ATT_SUPER),
        in_specs=[cur(0), prev(1), cur(1), prev(2), cur(2),
                  pl.BlockSpec(bias.shape, lambda b, n: (0, 0, 0, 0), pipeline_mode=pl.Buffered(1))],
        out_specs=[pl.BlockSpec((1, ATT_SUPER, ATT_GW), lambda b, n: (b, n, 0)),
                   pl.BlockSpec((1, ATT_HPG, ATT_SUPER, ATT_E), lambda b, n: (b, 0, n, 0))],
        out_shape=[jax.ShapeDtypeStruct((bsz, seq, ATT_GW), BF16),
                   jax.ShapeDtypeStruct((bsz, ATT_HPG, seq, ATT_E), F32)],
        scratch_shapes=[pltpu.VMEM((d, rows + blk, ATT_GW), BF16), pltpu.VMEM((d, rows + blk, ATT_GW), BF16),
                        pltpu.VMEM((ATT_HPG, ATT_SUPER if d > 1 else 8, ATT_E), F32)],
        compiler_params=_params("arbitrary", "arbitrary"),
        name=f"dilated_attn_g{group}",
    )(qkv, qkv, qkv, qkv, qkv, bias)


def _mix_kernel(ya_ref, o1_ref, o2_ref, o3_ref, l1_ref, l2_ref, l3_ref, ga_ref, gb_ref, x_ref,
                wa_ref, wb_ref, wo_ref, gt_ref, sc_ref, sh_ref, gn_ref, x1_ref, h2_ref):
    parts = []
    for j in range(ATT_HPG):
        sl = slice(j * ATT_E, (j + 1) * ATT_E)
        l1, l2, l3 = l1_ref[0, j], l2_ref[0, j], l3_ref[0, j]
        m = jnp.maximum(jnp.maximum(l1, l2), l3)
        e1, e2, e3 = jnp.exp2(l1 - m), jnp.exp2(l2 - m), jnp.exp2(l3 - m)
        num = (e1 * o1_ref[0, :, sl].astype(F32) + e2 * o2_ref[0, :, sl].astype(F32)
               + e3 * o3_ref[0, :, sl].astype(F32))
        parts.append(num / (e1 + e2 + e3))
    yb = jnp.concatenate(parts, axis=1)
    za = _dot(ya_ref[0], wa_ref[...])
    zb = _dot(yb.astype(BF16), wb_ref[...])
    ta = jnp.tanh((0.5 * ga_ref[0]).astype(F32))
    tb = jnp.tanh((0.5 * gb_ref[0]).astype(F32))
    merged2 = (ta + 1.0) * za + (tb + 1.0) * zb
    x1 = x_ref[0] + (0.5 * (1.0 + gt_ref[0])) * _dot(merged2.astype(BF16), wo_ref[...])
    x1_ref[0] = x1
    ms = jnp.mean(x1 * x1, axis=-1, keepdims=True)
    scale = gn_ref[...] * (1.0 + sc_ref[0])
    h2_ref[0] = (x1 * lax.rsqrt(ms + EPS) * scale + sh_ref[0]).astype(h2_ref.dtype)


def _mix(ya, outs, lses, main, x, wa, wb, wo, gt, sc, sh, gn):
    bsz, seq, d = x.shape
    tm = 512

    def tok(width, colb=0):
        return pl.BlockSpec((1, tm, width), lambda b, i, colb=colb: (b, i, colb))

    def full(shape):
        return pl.BlockSpec(shape, lambda b, i: (0,) * len(shape), pipeline_mode=pl.Buffered(1))

    def per_batch():
        return pl.BlockSpec((1, 1, d), lambda b, i: (b, 0, 0))

    lse_spec = pl.BlockSpec((1, ATT_HPG, tm, ATT_E), lambda b, i: (b, 0, i, 0))
    return pl.pallas_call(
        _mix_kernel,
        grid=(bsz, seq // tm),
        in_specs=[tok(HG_WIDTH)] + [tok(ATT_GW)] * 3 + [lse_spec] * 3 + [tok(d, 0), tok(d, 1), tok(d),
                  full(wa.shape), full(wb.shape), full(wo.shape),
                  per_batch(), per_batch(), per_batch(), full((1, d))],
        out_specs=[tok(d), tok(d)],
        out_shape=[jax.ShapeDtypeStruct((bsz, seq, d), F32), jax.ShapeDtypeStruct((bsz, seq, d), BF16)],
        compiler_params=_params("arbitrary", "arbitrary"),
        name="mix",
    )(ya, *outs, *lses, main, main, x, wa, wb, wo, gt, sc, sh, gn.reshape(1, d))


def _ffn_kernel(h_ref, x_ref, wu_ref, wd_ref, cw_ref, cb_ref, gt_ref, gf_ref, o_ref, carry_scr, ug_scr, uv_scr,
                act_scr, *, final_norm):
    tm = h_ref.shape[1]
    fc = FFN_CHUNK

    @pl.when(pl.program_id(1) == 0)
    def _():
        carry_scr[...] = jnp.zeros_like(carry_scr)

    h = h_ref[0]
    row = lax.broadcasted_iota(jnp.int32, (tm, fc), 0)
    n_chunks = D_FF // fc

    def up_project(ci):
        ug_scr[ci % 2] = _dot(h, wu_ref[:, ci * fc:(ci + 1) * fc])
        uv_scr[ci % 2] = _dot(h, wu_ref[:, D_FF + ci * fc:D_FF + (ci + 1) * fc])

    up_project(0)
    for ci in range(n_chunks):
        if ci + 1 < n_chunks:
            up_project(ci + 1)
        cs = slice(ci * fc, (ci + 1) * fc)
        ug = ug_scr[ci % 2]
        prev2 = jnp.broadcast_to(carry_scr[6:7, cs], (tm, fc))
        prev1 = jnp.broadcast_to(carry_scr[7:8, cs], (tm, fc))
        r1 = pltpu.roll(ug, 1, 0)
        r2 = pltpu.roll(ug, 2, 0)
        u1 = jnp.where(row == 0, prev1, r1)
        u2 = jnp.where(row == 0, prev2, jnp.where(row == 1, prev1, r2))
        carry_scr[:, cs] = ug[tm - 8:tm, :]
        y = cb_ref[:, cs] + u2 * cw_ref[0:1, cs] + u1 * cw_ref[1:2, cs] + ug * cw_ref[2:3, cs]
        gel = 0.5 * y * (1.0 + jnp.tanh(0.7978845608028654 * (y + 0.044715 * (y * y * y))))
        act_scr[:, cs] = (gel * uv_scr[ci % 2]).astype(BF16)
    x2 = x_ref[0] + (1.0 + gt_ref[0]) * _dot(act_scr[...], wd_ref[...])
    if final_norm:
        ms = jnp.mean(x2 * x2, axis=-1, keepdims=True)
        x2 = x2 * lax.rsqrt(ms + EPS) * gf_ref[...]
    o_ref[0] = x2


def _ffn(h2, x1, wu, wd, cw, cb, gt, gf, final_norm):
    bsz, seq, d = x1.shape
    tm = 1024

    def tok():
        return pl.BlockSpec((1, tm, d), lambda b, i: (b, i, 0))

    def full(shape):
        return pl.BlockSpec(shape, lambda b, i: (0,) * len(shape), pipeline_mode=pl.Buffered(1))

    return pl.pallas_call(
        functools.partial(_ffn_kernel, final_norm=final_norm),
        grid=(bsz, seq // tm),
        in_specs=[tok(), tok(), full(wu.shape), full(wd.shape), full(cw.shape), full((1, D_FF)),
                  pl.BlockSpec((1, 1, d), lambda b, i: (b, 0, 0)), full((1, d))],
        out_specs=tok(),
        out_shape=jax.ShapeDtypeStruct((bsz, seq, d), F32),
        scratch_shapes=[pltpu.VMEM((8, D_FF), F32), pltpu.VMEM((2, tm, FFN_CHUNK), F32),
                        pltpu.VMEM((2, tm, FFN_CHUNK), F32), pltpu.VMEM((tm, D_FF), BF16)],
        compiler_params=_params("arbitrary", "arbitrary"),
        name="ffn",
    )(h2, x1, wu, wd, cw, cb.reshape(1, D_FF), gt, gf.reshape(1, d))


def kernel(x, c, w_ada, b_ada, g_norm_mix, w_in, lb_logits, g_hg_norm, w_branch_a, w_branch_b, w_out, g_norm_ffn,
           w_up, conv_w, conv_b, w_down, g_final):
    depth = w_ada.shape[0]
    bsz = x.shape[0]
    n_hg = 4 * HG_WIDTH
    for l in range(depth):
        mod = _modulation(c, w_ada[l], b_ada[l]).reshape(bsz, N_MOD, 1, D_MODEL)
        sh1, sc1, gt1, sh2, sc2, gt2 = [mod[:, k] for k in range(N_MOD)]
        w = w_in[l]
        n_grp = len(ATT_GROUPS)
        o_q = n_hg
        o_g = n_hg + 3 * n_grp * ATT_GW
        cols = [w[:, o_g:], w[:, :n_hg]]
        for g in range(n_grp):
            cols += [w[:, o_q + (t * n_grp + g) * ATT_GW: o_q + (t * n_grp + g + 1) * ATT_GW] for t in range(3)]
        w_perm = jnp.concatenate(cols, axis=1).astype(BF16)

        gates, *qkvs, ya = _in_projection(x, g_norm_mix[l], sc1, sh1, w_perm, lb_logits,
                                          g_hg_norm[l].reshape(HG_WIDTH), l)
        groups = [_dilated_group(qkvs[g], g) for g in range(n_grp)]
        x1, h2 = _mix(ya, [o for o, _ in groups], [s for _, s in groups], gates, x,
                      w_branch_a[l].astype(BF16), w_branch_b[l].astype(BF16), w_out[l].astype(BF16),
                      gt1, sc2, sh2, g_norm_ffn[l])
        x = _ffn(h2, x1, w_up[l].astype(BF16), w_down[l].astype(BF16), conv_w[l], conv_b[l], gt2, g_final,
                 final_norm=(l == depth - 1))
    return x
```
